```python
import math
import jax, jax.numpy as jnp
from jax import lax
import numpy as np

D_MODEL = 1024
BATCH = 8
SEQ = 4096
DEPTH = 2

HEAD_DIM = 128
RET_HEADS = 4
MOBA_HEADS = 4
FOX_HEADS = 8
RET_WIDTH = RET_HEADS * HEAD_DIM
MOBA_WIDTH = MOBA_HEADS * HEAD_DIM
FOX_WIDTH = FOX_HEADS * HEAD_DIM
RET_CHUNK = 128
MOBA_BLOCK = 256
MOBA_TOPK = 3
MOBA_Q_CHUNK = 32
FOX_Q_BLOCK = 128
N_REL_BUCKETS = 32
REL_MAX_DISTANCE = 128
ROPE_BASE = 10000.0
RMS_EPS = 1e-6
GN_EPS = 1e-5
NEG = -1e30
FFN_HIDDEN = ((8 * D_MODEL + 3 * 256 - 1) // (3 * 256)) * 256
N_EVEN = (DEPTH + 1) // 2
N_ODD = DEPTH // 2
EVEN_IN = 4 * RET_WIDTH + 3 * MOBA_WIDTH
ODD_IN = 3 * FOX_WIDTH + FOX_HEADS

kernel_name = "hybrid_retention_moba_fox_block"


def rms_norm(x, g):
    xf = x.astype(jnp.float32)
    y = xf * lax.rsqrt(jnp.mean(xf * xf, axis=-1, keepdims=True) + RMS_EPS)
    return (y * g.astype(jnp.float32)).astype(x.dtype)


def split_heads(t, n_heads):
    b, s, _ = t.shape
    return t.reshape(b, s, n_heads, HEAD_DIM).transpose(0, 2, 1, 3).astype(jnp.float32)


def merge_heads(t):
    b, h, s, d = t.shape
    return t.transpose(0, 2, 1, 3).reshape(b, s, h * d)


def rotary(t, pos):
    d = t.shape[-1]
    inv_freq = jnp.power(ROPE_BASE, -jnp.arange(0, d, 2, dtype=jnp.float32) / d)
    ang = pos.astype(jnp.float32)[:, None] * inv_freq[None, :]
    cos, sin = jnp.cos(ang), jnp.sin(ang)
    t1, t2 = t[..., : d // 2], t[..., d // 2:]
    return jnp.concatenate([t1 * cos - t2 * sin, t1 * sin + t2 * cos], axis=-1)


def t5_bucket(rel):
    n = jnp.maximum(rel, 0)
    max_exact = N_REL_BUCKETS // 2
    nf = jnp.maximum(n, 1).astype(jnp.float32)
    large = max_exact + (jnp.log(nf / max_exact) / math.log(REL_MAX_DISTANCE / max_exact)
                         * (N_REL_BUCKETS - max_exact)).astype(jnp.int32)
    large = jnp.minimum(large, N_REL_BUCKETS - 1)
    return jnp.where(n < max_exact, n, large)


def retention(q, k, v, pos):
    b, h, s, d = q.shape
    c = RET_CHUNK
    n = s // c
    log_g = jnp.log1p(-jnp.power(2.0, -5.0 - jnp.arange(h, dtype=jnp.float32)))
    q = rotary(q, pos) * (d ** -0.5)
    k = rotary(k, pos)
    idx = jnp.arange(c, dtype=jnp.float32)
    diff = idx[:, None] - idx[None, :]
    inner = jnp.where(diff >= 0, jnp.exp(log_g[:, None, None] * jnp.maximum(diff, 0.0)), 0.0)
    q_dec = jnp.exp(log_g[:, None] * (idx + 1.0))[:, :, None]
    k_dec = jnp.exp(log_g[:, None] * (c - 1.0 - idx))[:, :, None]
    chunk_dec = jnp.exp(log_g * c)[:, None, None]

    def to_chunks(t):
        return t.reshape(b, h, n, c, t.shape[-1]).transpose(2, 0, 1, 3, 4)

    def step(state, inp):
        qi, ki, vi = inp
        sc = jnp.einsum('bhqd,bhkd->bhqk', qi, ki) * inner
        o = (jnp.einsum('bhqk,bhkv->bhqv', sc, vi)
             + jnp.einsum('bhqd,bhdv->bhqv', qi, state) * q_dec)
        state = state * chunk_dec + jnp.einsum('bhkd,bhkv->bhdv', ki * k_dec, vi)
        return state, o

    state0 = jnp.zeros((b, h, d, v.shape[-1]), jnp.float32)
    _, o = lax.scan(step, state0, (to_chunks(q), to_chunks(k), to_chunks(v)))
    return o.transpose(1, 2, 0, 3, 4).reshape(b, h, s, v.shape[-1])


def head_group_norm(o, g):
    mu = jnp.mean(o, axis=-1, keepdims=True)
    var = jnp.mean(jnp.square(o - mu), axis=-1, keepdims=True)
    y = (o - mu) * lax.rsqrt(var + GN_EPS)
    return merge_heads(y) * g.astype(jnp.float32)


def moba_attention(q, k, v, rel_bias, pos):
    b, h, s, d = q.shape
    L = MOBA_BLOCK
    nb = -(-s // L)
    pad = nb * L - s
    kb = jnp.pad(k, ((0, 0), (0, 0), (0, pad), (0, 0))).reshape(b, h, nb, L, d)
    vb = jnp.pad(v, ((0, 0), (0, 0), (0, pad), (0, 0))).reshape(b, h, nb, L, d)
    kmean = jnp.mean(kb, axis=3)
    topk = min(MOBA_TOPK, nb)
    scale = d ** -0.5
    bias_t = rel_bias.astype(jnp.float32).T
    qc_len = MOBA_Q_CHUNK
    n_qc = s // qc_len
    qc = q.reshape(b, h, n_qc, qc_len, d).transpose(2, 0, 1, 3, 4)
    bi = jnp.arange(b)[:, None, None, None]
    hi = jnp.arange(h)[None, :, None, None]
    blk_ids = jnp.arange(nb)
    offs = jnp.arange(L)

    def one_chunk(args):
        qi, ci = args
        q_pos = pos[0] + ci * qc_len + jnp.arange(qc_len)
        own = (ci * qc_len) // L
        gate = jnp.einsum('bhqd,bhnd->bhqn', qi, kmean)
        gate = jnp.where(blk_ids < own, gate, -jnp.inf)
        _, sel = lax.top_k(gate, topk)
        valid = sel < own
        ks = kb[bi, hi, sel]
        vs = vb[bi, hi, sel]
        s_sel = jnp.einsum('bhqd,bhqnld->bhqnl', qi, ks) * scale
        k_pos_sel = pos[0] + sel[..., None] * L + offs
        rel_sel = q_pos[None, None, :, None, None] - k_pos_sel
        s_sel = s_sel + bias_t[hi[..., None], t5_bucket(rel_sel)]
        s_sel = jnp.where(valid[..., None], s_sel, NEG).reshape(b, h, qc_len, topk * L)
        k_own = lax.dynamic_slice_in_dim(kb, own, 1, axis=2)[:, :, 0]
        v_own = lax.dynamic_slice_in_dim(vb, own, 1, axis=2)[:, :, 0]
        rel_own = q_pos[:, None] - (pos[0] + own * L + offs)[None, :]
        s_own = jnp.einsum('bhqd,bhld->bhql', qi, k_own) * scale + bias_t[:, t5_bucket(rel_own)]
        s_own = jnp.where(rel_own >= 0, s_own, NEG)
        p = jax.nn.softmax(jnp.concatenate([s_sel, s_own], axis=-1), axis=-1)
        p_sel = p[..., : topk * L].reshape(b, h, qc_len, topk, L)
        p_own = p[..., topk * L:]
        return (jnp.einsum('bhqnl,bhqnld->bhqd', p_sel, vs)
                + jnp.einsum('bhql,bhld->bhqd', p_own, v_own))

    o = lax.map(one_chunk, (qc, jnp.arange(n_qc)))
    return o.transpose(1, 2, 0, 3, 4).reshape(b, h, s, d)


def forgetting_attention(q, k, v, log_f):
    b, h, s, d = q.shape
    qb_len = FOX_Q_BLOCK
    n_qb = s // qb_len
    scale = d ** -0.5
    F = jnp.cumsum(log_f, axis=-1)
    qb = q.reshape(b, h, n_qb, qb_len, d).transpose(2, 0, 1, 3, 4)
    Fb = F.reshape(b, h, n_qb, qb_len).transpose(2, 0, 1, 3)
    k_pos = jnp.arange(s)

    def one_block(args):
        qi, Fi, blk = args
        q_pos = blk * qb_len + jnp.arange(qb_len)
        sc = (jnp.einsum('bhqd,bhkd->bhqk', qi, k) * scale
              + Fi[..., :, None] - F[:, :, None, :])
        sc = jnp.where(k_pos[None, :] <= q_pos[:, None], sc, NEG)
        p = jax.nn.softmax(sc, axis=-1)
        return jnp.einsum('bhqk,bhkd->bhqd', p, v)

    o = lax.map(one_block, (qb, Fb, jnp.arange(n_qb)))
    return o.transpose(1, 2, 0, 3, 4).reshape(b, h, s, d)


def even_mixer(h, w_in, gn_w, w_out, rel_bias, pos):
    proj = h @ w_in
    cuts = [RET_WIDTH * i for i in range(1, 5)] + [4 * RET_WIDTH + MOBA_WIDTH * i for i in range(1, 3)]
    rq, rk, rv, rg, mq, mk, mv = jnp.split(proj, cuts, axis=-1)
    ret = retention(split_heads(rq, RET_HEADS), split_heads(rk, RET_HEADS),
                    split_heads(rv, RET_HEADS), pos)
    ret = head_group_norm(ret, gn_w) * jax.nn.silu(rg.astype(jnp.float32))
    mob = merge_heads(moba_attention(split_heads(mq, MOBA_HEADS), split_heads(mk, MOBA_HEADS),
                                     split_heads(mv, MOBA_HEADS), rel_bias, pos))
    cat = jnp.concatenate([ret, mob], axis=-1).astype(h.dtype)
    return cat @ w_out


def odd_mixer(h, w_in, b_forget, w_out):
    proj = h @ w_in
    fq, fk, fv, fl = jnp.split(proj, [FOX_WIDTH, 2 * FOX_WIDTH, 3 * FOX_WIDTH], axis=-1)
    log_f = jax.nn.log_sigmoid(fl.astype(jnp.float32) + b_forget.astype(jnp.float32))
    o = forgetting_attention(split_heads(fq, FOX_HEADS), split_heads(fk, FOX_HEADS),
                             split_heads(fv, FOX_HEADS), log_f.transpose(0, 2, 1))
    return merge_heads(o).astype(h.dtype) @ w_out


def swiglu(h, w_in, w_out):
    gu = h @ w_in
    g, u = jnp.split(gu, [FFN_HIDDEN], axis=-1)
    return (jax.nn.silu(g) * u) @ w_out


def setup_inputs(seed: int = 0) -> dict:
    key = jax.random.key(seed)
    ks = jax.random.split(key, 15)
    f32 = jnp.float32

    def w(k, shape, fan_in):
        return jax.random.normal(k, shape, f32) * (fan_in ** -0.5)

    def gain(k, shape):
        return 1.0 + 0.05 * jax.random.normal(k, shape, f32)

    return {
        "x": jax.random.normal(ks[0], (BATCH, SEQ, D_MODEL), f32),
        "rel_bias": 0.5 * jax.random.normal(ks[1], (N_REL_BUCKETS, MOBA_HEADS), f32),
        "norm_mix_pre": gain(ks[2], (DEPTH, D_MODEL)),
        "norm_mix_post": gain(ks[3], (DEPTH, D_MODEL)),
        "norm_ffn_pre": gain(ks[4], (DEPTH, D_MODEL)),
        "norm_ffn_post": gain(ks[5], (DEPTH, D_MODEL)),
        "even_w_in": w(ks[6], (N_EVEN, D_MODEL, EVEN_IN), D_MODEL),
        "even_gn": gain(ks[7], (N_EVEN, RET_WIDTH)),
        "even_w_out": w(ks[8], (N_EVEN, RET_WIDTH + MOBA_WIDTH, D_MODEL), RET_WIDTH + MOBA_WIDTH),
        "odd_w_in": w(ks[9], (N_ODD, D_MODEL, ODD_IN), D_MODEL),
        "odd_b_forget": 2.0 + 0.5 * jax.random.normal(ks[10], (N_ODD, FOX_HEADS), f32),
        "odd_w_out": w(ks[11], (N_ODD, FOX_WIDTH, D_MODEL), FOX_WIDTH),
        "ffn_w_in": w(ks[12], (DEPTH, D_MODEL, 2 * FFN_HIDDEN), D_MODEL),
        "ffn_w_out": w(ks[13], (DEPTH, FFN_HIDDEN, D_MODEL), FFN_HIDDEN),
    }


def reference(x, rel_bias, norm_mix_pre, norm_mix_post, norm_ffn_pre, norm_ffn_post,
              even_w_in, even_gn, even_w_out, odd_w_in, odd_b_forget, odd_w_out,
              ffn_w_in, ffn_w_out):
    pos = jnp.arange(x.shape[1], dtype=jnp.int32)
    for layer in range(DEPTH):
        i = layer // 2
        hn = rms_norm(x, norm_mix_pre[layer])
        if layer % 2 == 0:
            m = even_mixer(hn, even_w_in[i], even_gn[i], even_w_out[i], rel_bias, pos)
        else:
            m = odd_mixer(hn, odd_w_in[i], odd_b_forget[i], odd_w_out[i])
        x = x + rms_norm(m, norm_mix_post[layer])
        hn = rms_norm(x, norm_ffn_pre[layer])
        x = x + rms_norm(swiglu(hn, ffn_w_in[layer], ffn_w_out[layer]), norm_ffn_post[layer])
    return x
```

```python
import functools
import math

import jax
import jax.numpy as jnp
from jax import lax
from jax.experimental import pallas as pl
from jax.experimental.pallas import tpu as pltpu

F32 = jnp.float32
BF16 = jnp.bfloat16

HEAD_DIM = 128
RET_HEADS = 4
MOBA_HEADS = 4
FOX_HEADS = 8
RET_CHUNK = 128
MOBA_BLOCK = 256
MOBA_TOPK = 3
N_REL_BUCKETS = 32
REL_MAX_DISTANCE = 128
ROPE_BASE = 10000.0
RMS_EPS = 1e-6
GN_EPS = 1e-5
NEG = -1e30

V7X_VMEM_BYTES = 64 * 1024 * 1024
VMEM_LIMIT = V7X_VMEM_BYTES - 8 * 1024 * 1024

ROW_TILE = 512
COL_CHUNK = 512
FFN_CHUNK = 256
FOX_TILE = 512


def _params(*sem):
    return pltpu.CompilerParams(dimension_semantics=sem, vmem_limit_bytes=VMEM_LIMIT)


def _resident(shape):
    nd = len(shape)
    return pl.BlockSpec(shape, lambda *_: (0,) * nd, pipeline_mode=pl.Buffered(1))


def _rms(x, g):
    return x * lax.rsqrt(jnp.mean(x * x, axis=-1, keepdims=True) + RMS_EPS) * g


def _dot(a, b):
    return jnp.dot(a, b, preferred_element_type=F32)


def _dot_nt(a, b):
    return lax.dot_general(a, b, (((1,), (1,)), ((), ())), preferred_element_type=F32)


def _norm_proj_kernel(x_ref, g_ref, w_ref, o_ref):
    hn = _rms(x_ref[...], g_ref[...]).astype(BF16)
    for c in range(o_ref.shape[1] // COL_CHUNK):
        sl = slice(c * COL_CHUNK, (c + 1) * COL_CHUNK)
        o_ref[:, sl] = _dot(hn, w_ref[:, sl]).astype(o_ref.dtype)


def _norm_proj(x, g, w):
    t, d = x.shape
    n = w.shape[1]
    return pl.pallas_call(
        _norm_proj_kernel,
        grid=(t // ROW_TILE,),
        in_specs=[pl.BlockSpec((ROW_TILE, d), lambda i: (i, 0)),
                  _resident((1, d)), _resident((d, n))],
        out_specs=pl.BlockSpec((ROW_TILE, n), lambda i: (i, 0)),
        out_shape=jax.ShapeDtypeStruct((t, n), BF16),
        compiler_params=_params("parallel"),
        name="norm_proj",
    )(x, g, w)


def _norm_proj_gate_kernel(x_ref, g_ref, w_ref, wf_ref, o_ref, f_ref):
    hn32 = _rms(x_ref[...], g_ref[...])
    hn = hn32.astype(BF16)
    for c in range(o_ref.shape[1] // COL_CHUNK):
        sl = slice(c * COL_CHUNK, (c + 1) * COL_CHUNK)
        o_ref[:, sl] = _dot(hn, w_ref[:, sl]).astype(o_ref.dtype)
    f_ref[...] = lax.dot_general(wf_ref[...], hn32, (((1,), (1,)), ((), ())),
                                 preferred_element_type=F32,
                                 precision=lax.Precision.HIGHEST)


def _norm_proj_gate(x, g, w, wf_t):
    t, d = x.shape
    n = w.shape[1]
    nh = wf_t.shape[0]
    return pl.pallas_call(
        _norm_proj_gate_kernel,
        grid=(t // ROW_TILE,),
        in_specs=[pl.BlockSpec((ROW_TILE, d), lambda i: (i, 0)),
                  _resident((1, d)), _resident((d, n)), _resident((nh, d))],
        out_specs=[pl.BlockSpec((ROW_TILE, n), lambda i: (i, 0)),
                   pl.BlockSpec((nh, ROW_TILE), lambda i: (0, i))],
        out_shape=[jax.ShapeDtypeStruct((t, n), BF16),
                   jax.ShapeDtypeStruct((nh, t), F32)],
        compiler_params=_params("parallel"),
        name="norm_proj_gate",
    )(x, g, w, wf_t)


def _out_proj_kernel(n_in, *refs):
    a_refs = refs[:n_in]
    w_refs = refs[n_in:2 * n_in]
    g_ref, x_ref, o_ref = refs[2 * n_in:]
    m = _dot(a_refs[0][...], w_refs[0][...])
    for a_ref, w_ref in zip(a_refs[1:], w_refs[1:]):
        m = m + _dot(a_ref[...], w_ref[...])
    o_ref[...] = x_ref[...] + _rms(m, g_ref[...])


def _out_proj(acts, weights, g, x):
    t, d = x.shape
    n_in = len(acts)
    in_specs = [pl.BlockSpec((ROW_TILE, a.shape[1]), lambda i: (i, 0)) for a in acts]
    in_specs += [_resident(w.shape) for w in weights]
    in_specs += [_resident((1, d)), pl.BlockSpec((ROW_TILE, d), lambda i: (i, 0))]
    return pl.pallas_call(
        functools.partial(_out_proj_kernel, n_in),
        grid=(t // ROW_TILE,),
        in_specs=in_specs,
        out_specs=pl.BlockSpec((ROW_TILE, d), lambda i: (i, 0)),
        out_shape=jax.ShapeDtypeStruct((t, d), F32),
        compiler_params=_params("parallel"),
        name="out_proj",
    )(*acts, *weights, g, x)


def _ffn_kernel(x_ref, gpre_ref, wg_ref, wu_ref, wo_ref, gpost_ref, o_ref):
    x = x_ref[...]
    hn = _rms(x, gpre_ref[...]).astype(BF16)
    hidden = wg_ref.shape[1]
    y = jnp.zeros(x.shape, F32)
    for c in range(hidden // FFN_CHUNK):
        sl = slice(c * FFN_CHUNK, (c + 1) * FFN_CHUNK)
        gate = _dot(hn, wg_ref[:, sl])
        up = _dot(hn, wu_ref[:, sl])
        act = (gate * jax.nn.sigmoid(gate) * up).astype(BF16)
        y = y + _dot(act, wo_ref[sl, :])
    o_ref[...] = x + _rms(y, gpost_ref[...])


def _ffn(x, gpre, wg, wu, wo, gpost):
    t, d = x.shape
    return pl.pallas_call(
        _ffn_kernel,
        grid=(t // ROW_TILE,),
        in_specs=[pl.BlockSpec((ROW_TILE, d), lambda i: (i, 0)),
                  _resident((1, d)), _resident(wg.shape), _resident(wu.shape),
                  _resident(wo.shape), _resident((1, d))],
        out_specs=pl.BlockSpec((ROW_TILE, d), lambda i: (i, 0)),
        out_shape=jax.ShapeDtypeStruct((t, d), F32),
        compiler_params=_params("parallel"),
        name="ffn",
    )(x, gpre, wg, wu, wo, gpost)


def _retention_kernel(lg_ref, q_ref, k_ref, v_ref, g_ref, cos_ref, sin_ref, gn_ref, o_ref):
    c = RET_CHUNK
    seq = q_ref.shape[0]
    lg = lg_ref[pl.program_id(1)]
    row = lax.broadcasted_iota(jnp.int32, (c, c), 0).astype(F32)
    col = lax.broadcasted_iota(jnp.int32, (c, c), 1).astype(F32)
    diff = row - col
    inner = jnp.where(diff >= 0, jnp.exp(lg * jnp.maximum(diff, 0.0)), 0.0)
    idx = lax.broadcasted_iota(jnp.int32, (c, HEAD_DIM), 0).astype(F32)
    q_dec = jnp.exp(lg * (idx + 1.0))
    k_dec = jnp.exp(lg * (c - 1.0 - idx))
    chunk_dec = jnp.exp(lg * jnp.full((1, HEAD_DIM), float(c), F32))
    gn_w = gn_ref[...]
    scale = HEAD_DIM ** -0.5

    def rot(t, cos, sin):
        return t * cos + pltpu.roll(t, HEAD_DIM // 2, 1) * sin

    def step(n, state):
        sl = pl.ds(pl.multiple_of(n * c, c), c)
        cos = cos_ref[sl, :]
        sin = sin_ref[sl, :]
        q = rot(q_ref[sl, :].astype(F32), cos, sin) * scale
        k = rot(k_ref[sl, :].astype(F32), cos, sin)
        v = v_ref[sl, :]
        qb = q.astype(BF16)
        sc = _dot_nt(qb, k.astype(BF16)) * inner
        o = _dot(sc.astype(BF16), v) + _dot(qb, state.astype(BF16)) * q_dec
        kd_t = jnp.transpose(k * k_dec).astype(BF16)
        state = state * chunk_dec + _dot(kd_t, v)
        mu = jnp.mean(o, axis=-1, keepdims=True)
        var = jnp.mean(jnp.square(o - mu), axis=-1, keepdims=True)
        y = (o - mu) * lax.rsqrt(var + GN_EPS) * gn_w
        gate = g_ref[sl, :].astype(F32)
        o_ref[sl, :] = (y * (gate * jax.nn.sigmoid(gate))).astype(o_ref.dtype)
        return state

    lax.fori_loop(0, seq // c, step, jnp.zeros((HEAD_DIM, HEAD_DIM), F32))


def _retention(proj, log_g, cos2, sin2, gn_w, batch, seq):
    t = proj.shape[0]
    nh = RET_HEADS

    def head_spec(group):
        return pl.BlockSpec((seq, HEAD_DIM), lambda b, h: (b, group * nh + h))

    return pl.pallas_call(
        _retention_kernel,
        grid=(batch, nh),
        in_specs=[pl.BlockSpec(memory_space=pltpu.SMEM),
                  head_spec(0), head_spec(1), head_spec(2), head_spec(3),
                  _resident((seq, HEAD_DIM)), _resident((seq, HEAD_DIM)),
                  pl.BlockSpec((1, HEAD_DIM), lambda b, h: (0, h))],
        out_specs=pl.BlockSpec((seq, HEAD_DIM), lambda b, h: (b, h)),
        out_shape=jax.ShapeDtypeStruct((t, nh * HEAD_DIM), BF16),
        compiler_params=_params("parallel", "parallel"),
        name="retention",
    )(log_g, proj, proj, proj, proj, cos2, sin2, gn_w)


def _moba_kernel(b31_ref, q_ref, k_ref, v_ref, town_ref, tadj_ref, o_ref,
                 km_ref, vt_ref, sel_ref):
    blk = MOBA_BLOCK
    nb = k_ref.shape[0] // blk
    i = pl.program_id(2)
    b31 = b31_ref[pl.program_id(1)]

    @pl.when(i == 0)
    def _():
        for n in range(nb):
            kb = k_ref[n * blk:(n + 1) * blk, :].astype(F32)
            km_ref[n:n + 1, :] = jnp.mean(kb, axis=0, keepdims=True)
            vt_ref[n] = jnp.transpose(v_ref[n * blk:(n + 1) * blk, :].astype(F32)).astype(BF16)

    q = q_ref[...]
    km = km_ref[...]
    km_hi = km.astype(BF16)
    km_lo = (km - km_hi.astype(F32)).astype(BF16)
    gate = _dot_nt(km_hi, q) + _dot_nt(km_lo, q)
    row = lax.broadcasted_iota(jnp.int32, gate.shape, 0).astype(F32)
    valid = row < i.astype(F32)
    gm = jnp.where(valid, gate, -jnp.inf)
    sel = jnp.full(gate.shape, NEG, F32)
    for _ in range(min(MOBA_TOPK, nb)):
        mx = jnp.max(gm, axis=0, keepdims=True)
        first = jnp.min(jnp.where(gm == mx, row, float(nb)), axis=0, keepdims=True)
        pick = row == first
        sel = jnp.where(jnp.logical_and(pick, valid), 0.0, sel)
        gm = jnp.where(pick, -jnp.inf, gm)
    sel_ref[...] = sel

    qs = (q.astype(F32) * (HEAD_DIM ** -0.5)).astype(BF16)

    def tile(carry, j, bias):
        m, l, acc = carry
        kj = k_ref[pl.ds(pl.multiple_of(j * blk, blk), blk), :]
        s = _dot_nt(kj, qs) + bias
        m_new = jnp.maximum(m, jnp.max(s, axis=0, keepdims=True))
        alpha = jnp.exp(m - m_new)
        p = jnp.exp(s - m_new)
        l = alpha * l + jnp.sum(p, axis=0, keepdims=True)
        acc = alpha * acc + _dot(vt_ref[j], p.astype(BF16))
        return m_new, l, acc

    init = (jnp.full((1, blk), NEG, F32), jnp.zeros((1, blk), F32),
            jnp.zeros((HEAD_DIM, blk), F32))
    carry = tile(init, i, town_ref[...])
    ja = jnp.maximum(i - 1, 0)
    adj_row = jnp.where(i >= 1, sel_ref[pl.ds(ja, 1), :], NEG)
    carry = tile(carry, ja, tadj_ref[...] + adj_row)

    def far(j, carry):
        return tile(carry, j, sel_ref[pl.ds(j, 1), :] + b31)

    m, l, acc = lax.fori_loop(0, jnp.maximum(i - 1, 0), far, carry)
    o_ref[...] = jnp.transpose(acc / l).astype(o_ref.dtype)


def _moba(proj, b31, town_t, tadj_t, batch, seq, col0):
    t = proj.shape[0]
    nh = MOBA_HEADS
    blk = MOBA_BLOCK
    nb = seq // blk
    return pl.pallas_call(
        _moba_kernel,
        grid=(batch, nh, nb),
        in_specs=[pl.BlockSpec(memory_space=pltpu.SMEM),
                  pl.BlockSpec((blk, HEAD_DIM), lambda b, h, i: (b * nb + i, col0 + h)),
                  pl.BlockSpec((seq, HEAD_DIM), lambda b, h, i: (b, col0 + nh + h)),
                  pl.BlockSpec((seq, HEAD_DIM), lambda b, h, i: (b, col0 + 2 * nh + h)),
                  pl.BlockSpec((None, blk, blk), lambda b, h, i: (h, 0, 0)),
                  pl.BlockSpec((None, blk, blk), lambda b, h, i: (h, 0, 0))],
        out_specs=pl.BlockSpec((blk, HEAD_DIM), lambda b, h, i: (b * nb + i, h)),
        out_shape=jax.ShapeDtypeStruct((t, nh * HEAD_DIM), BF16),
        scratch_shapes=[pltpu.VMEM((nb, HEAD_DIM), F32),
                        pltpu.VMEM((nb, HEAD_DIM, blk), BF16),
                        pltpu.VMEM((nb, blk), F32)],
        compiler_params=_params("parallel", "parallel", "arbitrary"),
        name="moba",
    )(b31, proj, proj, proj, town_t, tadj_t)


def _fox_gate_kernel(fl_ref, bf_ref, o_ref):
    x = fl_ref[...] + bf_ref[...]
    y = jnp.minimum(x, 0.0) - jnp.log1p(jnp.exp(-jnp.abs(x)))
    n = y.shape[1]
    lane = lax.broadcasted_iota(jnp.int32, y.shape, 1)
    sh = 1
    while sh < n:
        y = y + jnp.where(lane >= sh, pltpu.roll(y, sh, 1), 0.0)
        sh *= 2
    o_ref[...] = y


def _fox_gate(fl_t, b_forget, batch, seq):
    nh = fl_t.shape[0]
    return pl.pallas_call(
        _fox_gate_kernel,
        grid=(batch,),
        in_specs=[pl.BlockSpec((nh, seq), lambda b: (0, b)), _resident((nh, 1))],
        out_specs=pl.BlockSpec((nh, seq), lambda b: (0, b)),
        out_shape=jax.ShapeDtypeStruct((nh, batch * seq), F32),
        compiler_params=_params("parallel"),
        name="fox_gate",
    )(fl_t, b_forget)


def _fox_kernel(q_ref, k_ref, v_ref, fq_ref, fk_ref, o_ref):
    tq = FOX_TILE
    i = pl.program_id(2)
    qs = (q_ref[...].astype(F32) * (HEAD_DIM ** -0.5)).astype(BF16)
    c = jnp.max(fq_ref[...], axis=1, keepdims=True)

    def tile(carry, j, causal):
        m, l, acc = carry
        sl = pl.ds(pl.multiple_of(j * tq, tq), tq)
        s = _dot_nt(qs, k_ref[sl, :]) + (c - fk_ref[j])
        if causal:
            r = lax.broadcasted_iota(jnp.int32, s.shape, 0)
            cc = lax.broadcasted_iota(jnp.int32, s.shape, 1)
            s = jnp.where(cc <= r, s, NEG)
        m_new = jnp.maximum(m, jnp.max(s, axis=1, keepdims=True))
        alpha = jnp.exp(m - m_new)
        p = jnp.exp(s - m_new)
        l = alpha * l + jnp.sum(p, axis=1, keepdims=True)
        acc = alpha * acc + _dot(p.astype(BF16), v_ref[sl, :])
        return m_new, l, acc

    init = (jnp.full((tq, 1), NEG, F32), jnp.zeros((tq, 1), F32),
            jnp.zeros((tq, HEAD_DIM), F32))
    carry = tile(init, i, True)
    m, l, acc = lax.fori_loop(0, i, lambda j, cr: tile(cr, j, False), carry)
    o_ref[...] = (acc / l).astype(o_ref.dtype)


def _fox(qkv, f_tiles, batch, seq):
    t = qkv.shape[0]
    nh = FOX_HEADS
    tq = FOX_TILE
    nq = seq // tq
    return pl.pallas_call(
        _fox_kernel,
        grid=(batch, nh, nq),
        in_specs=[pl.BlockSpec((tq, HEAD_DIM), lambda b, h, i: (b * nq + i, h)),
                  pl.BlockSpec((seq, HEAD_DIM), lambda b, h, i: (b, nh + h)),
                  pl.BlockSpec((seq, HEAD_DIM), lambda b, h, i: (b, 2 * nh + h)),
                  pl.BlockSpec((None, None, None, 1, tq), lambda b, h, i: (b, h, i, 0, 0)),
                  pl.BlockSpec((None, None, nq, 1, tq), lambda b, h, i: (b, h, 0, 0, 0))],
        out_specs=pl.BlockSpec((tq, HEAD_DIM), lambda b, h, i: (b * nq + i, h)),
        out_shape=jax.ShapeDtypeStruct((t, nh * HEAD_DIM), BF16),
        compiler_params=_params("parallel", "parallel", "arbitrary"),
        name="fox",
    )(qkv, qkv, qkv, f_tiles, f_tiles)


def _rotary_tables(seq):
    inv_freq = jnp.power(ROPE_BASE, -jnp.arange(0, HEAD_DIM, 2, dtype=F32) / HEAD_DIM)
    ang = jnp.arange(seq, dtype=F32)[:, None] * inv_freq[None, :]
    cos, sin = jnp.cos(ang), jnp.sin(ang)
    return jnp.concatenate([cos, cos], axis=-1), jnp.concatenate([-sin, sin], axis=-1)


def _t5_bucket(rel):
    n = jnp.maximum(rel, 0)
    max_exact = N_REL_BUCKETS // 2
    nf = jnp.maximum(n, 1).astype(F32)
    large = max_exact + (jnp.log(nf / max_exact) / math.log(REL_MAX_DISTANCE / max_exact)
                         * (N_REL_BUCKETS - max_exact)).astype(jnp.int32)
    large = jnp.minimum(large, N_REL_BUCKETS - 1)
    return jnp.where(n < max_exact, n, large)


def _moba_bias_tables(rel_bias):
    blk = MOBA_BLOCK
    bias_t = rel_bias.astype(F32).T
    kk = jnp.arange(blk)[:, None]
    qq = jnp.arange(blk)[None, :]
    own = jnp.where(qq >= kk, bias_t[:, _t5_bucket(qq - kk)], NEG)
    adj = bias_t[:, _t5_bucket(blk + qq - kk)]
    far = bias_t[:, _t5_bucket(jnp.array(2 * blk, jnp.int32))]
    return far, own, adj


def kernel(x, rel_bias, norm_mix_pre, norm_mix_post, norm_ffn_pre, norm_ffn_post,
           even_w_in, even_gn, even_w_out, odd_w_in, odd_b_forget, odd_w_out,
           ffn_w_in, ffn_w_out):
    batch, seq, d = x.shape
    depth = norm_mix_pre.shape[0]
    hidden = ffn_w_out.shape[1]
    ret_w = RET_HEADS * HEAD_DIM
    fox_w = FOX_HEADS * HEAD_DIM
    xt = x.reshape(batch * seq, d)

    cos2, sin2 = _rotary_tables(seq)
    log_g = jnp.log1p(-jnp.power(2.0, -5.0 - jnp.arange(RET_HEADS, dtype=F32)))
    b31, town_t, tadj_t = _moba_bias_tables(rel_bias)

    def row(v):
        return v.reshape(1, -1).astype(F32)

    for layer in range(depth):
        i = layer // 2
        if layer % 2 == 0:
            proj = _norm_proj(xt, row(norm_mix_pre[layer]), even_w_in[i].astype(BF16))
            ret = _retention(proj, log_g, cos2, sin2, row(even_gn[i]), batch, seq)
            mob = _moba(proj, b31, town_t, tadj_t, batch, seq, 4 * RET_HEADS)
            w_out = even_w_out[i].astype(BF16)
            xt = _out_proj([ret, mob], [w_out[:ret_w], w_out[ret_w:]],
                           row(norm_mix_post[layer]), xt)
        else:
            w_in = odd_w_in[i]
            qkv, fl_t = _norm_proj_gate(xt, row(norm_mix_pre[layer]),
                                        w_in[:, :3 * fox_w].astype(BF16),
                                        w_in[:, 3 * fox_w:].T.astype(F32))
            f = _fox_gate(fl_t, odd_b_forget[i].reshape(-1, 1).astype(F32), batch, seq)
            f_tiles = (f.reshape(FOX_HEADS, batch, seq // FOX_TILE, 1, FOX_TILE)
                       .transpose(1, 0, 2, 3, 4))
            o = _fox(qkv, f_tiles, batch, seq)
            xt = _out_proj([o], [odd_w_out[i].astype(BF16)], row(norm_mix_post[layer]), xt)
        w_ffn = ffn_w_in[layer].astype(BF16)
        xt = _ffn(xt, row(norm_ffn_pre[layer]), w_ffn[:, :hidden], w_ffn[:, hidden:],
                  ffn_w_out[layer].astype(BF16), row(norm_ffn_post[layer]))
    return xt.reshape(batch, seq, d)
```

```python
import functools
import math

import jax
import jax.numpy as jnp
from jax import lax
from jax.experimental import pallas as pl
from jax.experimental.pallas import tpu as pltpu

F32 = jnp.float32
BF16 = jnp.bfloat16

HEAD_DIM = 128
RET_HEADS = 4
MOBA_HEADS = 4
FOX_HEADS = 8
RET_CHUNK = 128
MOBA_BLOCK = 256
MOBA_TOPK = 3
N_REL_BUCKETS = 32
REL_MAX_DISTANCE = 128
ROPE_BASE = 10000.0
RMS_EPS = 1e-6
GN_EPS = 1e-5
NEG = -1e30
LOG2E = math.log2(math.e)

V7X_VMEM_BYTES = 64 * 1024 * 1024
VMEM_LIMIT = V7X_VMEM_BYTES - 8 * 1024 * 1024

ROW_TILE = 512
COL_CHUNK = 512
FFN_CHUNK = 256
FOX_TILE = 512


def _params(*sem):
    return pltpu.CompilerParams(dimension_semantics=sem, vmem_limit_bytes=VMEM_LIMIT)


def _resident(shape):
    nd = len(shape)
    return pl.BlockSpec(shape, lambda *_: (0,) * nd, pipeline_mode=pl.Buffered(1))


def _rms(x, g):
    return x * lax.rsqrt(jnp.mean(x * x, axis=-1, keepdims=True) + RMS_EPS) * g


def _dot(a, b):
    return jnp.dot(a, b, preferred_element_type=F32)


def _dot_nt(a, b):
    return lax.dot_general(a, b, (((1,), (1,)), ((), ())), preferred_element_type=F32)


def _norm_proj_kernel(x_ref, g_ref, w_ref, o_ref):
    hn = _rms(x_ref[...], g_ref[...]).astype(BF16)
    for c in range(o_ref.shape[1] // COL_CHUNK):
        sl = slice(c * COL_CHUNK, (c + 1) * COL_CHUNK)
        o_ref[:, sl] = _dot(hn, w_ref[:, sl]).astype(o_ref.dtype)


def _norm_proj(x, g, w):
    t, d = x.shape
    n = w.shape[1]
    return pl.pallas_call(
        _norm_proj_kernel,
        grid=(t // ROW_TILE,),
        in_specs=[pl.BlockSpec((ROW_TILE, d), lambda i: (i, 0)),
                  _resident((1, d)), _resident((d, n))],
        out_specs=pl.BlockSpec((ROW_TILE, n), lambda i: (i, 0)),
        out_shape=jax.ShapeDtypeStruct((t, n), BF16),
        compiler_params=_params("parallel"),
        name="norm_proj",
    )(x, g, w)


def _norm_proj_gate_kernel(x_ref, g_ref, w_ref, wf_ref, o_ref, f_ref):
    hn32 = _rms(x_ref[...], g_ref[...])
    hn = hn32.astype(BF16)
    for c in range(o_ref.shape[1] // COL_CHUNK):
        sl = slice(c * COL_CHUNK, (c + 1) * COL_CHUNK)
        o_ref[:, sl] = _dot(hn, w_ref[:, sl]).astype(o_ref.dtype)
    f_ref[...] = lax.dot_general(wf_ref[...], hn32, (((1,), (1,)), ((), ())),
                                 preferred_element_type=F32,
                                 precision=lax.Precision.HIGHEST)


def _norm_proj_gate(x, g, w, wf_t):
    t, d = x.shape
    n = w.shape[1]
    nh = wf_t.shape[0]
    return pl.pallas_call(
        _norm_proj_gate_kernel,
        grid=(t // ROW_TILE,),
        in_specs=[pl.BlockSpec((ROW_TILE, d), lambda i: (i, 0)),
                  _resident((1, d)), _resident((d, n)), _resident((nh, d))],
        out_specs=[pl.BlockSpec((ROW_TILE, n), lambda i: (i, 0)),
                   pl.BlockSpec((nh, ROW_TILE), lambda i: (0, i))],
        out_shape=[jax.ShapeDtypeStruct((t, n), BF16),
                   jax.ShapeDtypeStruct((nh, t), F32)],
        compiler_params=_params("parallel"),
        name="norm_proj_gate",
    )(x, g, w, wf_t)


def _out_proj_kernel(n_in, *refs):
    a_refs = refs[:n_in]
    w_refs = refs[n_in:2 * n_in]
    g_ref, x_ref, o_ref = refs[2 * n_in:]
    m = _dot(a_refs[0][...], w_refs[0][...])
    for a_ref, w_ref in zip(a_refs[1:], w_refs[1:]):
        m = m + _dot(a_ref[...], w_ref[...])
    o_ref[...] = x_ref[...] + _rms(m, g_ref[...])


def _out_proj(acts, weights, g, x):
    t, d = x.shape
    n_in = len(acts)
    in_specs = [pl.BlockSpec((ROW_TILE, a.shape[1]), lambda i: (i, 0)) for a in acts]
    in_specs += [_resident(w.shape) for w in weights]
    in_specs += [_resident((1, d)), pl.BlockSpec((ROW_TILE, d), lambda i: (i, 0))]
    return pl.pallas_call(
        functools.partial(_out_proj_kernel, n_in),
        grid=(t // ROW_TILE,),
        in_specs=in_specs,
        out_specs=pl.BlockSpec((ROW_TILE, d), lambda i: (i, 0)),
        out_shape=jax.ShapeDtypeStruct((t, d), F32),
        compiler_params=_params("parallel"),
        name="out_proj",
    )(*acts, *weights, g, x)


def _ffn_kernel(x_ref, gpre_ref, wg_ref, wu_ref, wo_ref, gpost_ref, o_ref):
    x = x_ref[...]
    hn = _rms(x, gpre_ref[...]).astype(BF16)
    hidden = wg_ref.shape[1]
    y = jnp.zeros(x.shape, F32)
    for c in range(hidden // FFN_CHUNK):
        sl = slice(c * FFN_CHUNK, (c + 1) * FFN_CHUNK)
        gate = _dot(hn, wg_ref[:, sl])
        up = _dot(hn, wu_ref[:, sl])
        act = (gate * jax.nn.sigmoid(gate) * up).astype(BF16)
        y = y + _dot(act, wo_ref[sl, :])
    o_ref[...] = x + _rms(y, gpost_ref[...])


def _ffn(x, gpre, wg, wu, wo, gpost):
    t, d = x.shape
    return pl.pallas_call(
        _ffn_kernel,
        grid=(t // ROW_TILE,),
        in_specs=[pl.BlockSpec((ROW_TILE, d), lambda i: (i, 0)),
                  _resident((1, d)), _resident(wg.shape), _resident(wu.shape),
                  _resident(wo.shape), _resident((1, d))],
        out_specs=pl.BlockSpec((ROW_TILE, d), lambda i: (i, 0)),
        out_shape=jax.ShapeDtypeStruct((t, d), F32),
        compiler_params=_params("parallel"),
        name="ffn",
    )(x, gpre, wg, wu, wo, gpost)


def _retention_kernel(lg_ref, q_ref, k_ref, v_ref, g_ref, cos_ref, sin_ref, gn_ref, o_ref):
    c = RET_CHUNK
    seq = q_ref.shape[0]
    lg = lg_ref[pl.program_id(1)]
    row = lax.broadcasted_iota(jnp.int32, (c, c), 0).astype(F32)
    col = lax.broadcasted_iota(jnp.int32, (c, c), 1).astype(F32)
    diff = row - col
    inner = jnp.where(diff >= 0, jnp.exp(lg * jnp.maximum(diff, 0.0)), 0.0)
    idx = lax.broadcasted_iota(jnp.int32, (c, HEAD_DIM), 0).astype(F32)
    q_dec = jnp.exp(lg * (idx + 1.0))
    k_dec = jnp.exp(lg * (c - 1.0 - idx))
    chunk_dec = jnp.exp(lg * jnp.full((1, HEAD_DIM), float(c), F32))
    gn_w = gn_ref[...]
    scale = HEAD_DIM ** -0.5

    def rot(t, cos, sin):
        return t * cos + pltpu.roll(t, HEAD_DIM // 2, 1) * sin

    def step(n, state):
        sl = pl.ds(pl.multiple_of(n * c, c), c)
        cos = cos_ref[sl, :]
        sin = sin_ref[sl, :]
        q = rot(q_ref[sl, :].astype(F32), cos, sin) * scale
        k = rot(k_ref[sl, :].astype(F32), cos, sin)
        v = v_ref[sl, :]
        qb = q.astype(BF16)
        sc = _dot_nt(qb, k.astype(BF16)) * inner
        o = _dot(sc.astype(BF16), v) + _dot(qb, state.astype(BF16)) * q_dec
        kd_t = jnp.transpose(k * k_dec).astype(BF16)
        state = state * chunk_dec + _dot(kd_t, v)
        mu = jnp.mean(o, axis=-1, keepdims=True)
        var = jnp.mean(jnp.square(o - mu), axis=-1, keepdims=True)
        y = (o - mu) * lax.rsqrt(var + GN_EPS) * gn_w
        gate = g_ref[sl, :].astype(F32)
        o_ref[sl, :] = (y * (gate * jax.nn.sigmoid(gate))).astype(o_ref.dtype)
        return state

    lax.fori_loop(0, seq // c, step, jnp.zeros((HEAD_DIM, HEAD_DIM), F32))


def _retention(proj, log_g, cos2, sin2, gn_w, batch, seq):
    t = proj.shape[0]
    nh = RET_HEADS

    def head_spec(group):
        return pl.BlockSpec((seq, HEAD_DIM), lambda b, h: (b, group * nh + h))

    return pl.pallas_call(
        _retention_kernel,
        grid=(batch, nh),
        in_specs=[pl.BlockSpec(memory_space=pltpu.SMEM),
                  head_spec(0), head_spec(1), head_spec(2), head_spec(3),
                  _resident((seq, HEAD_DIM)), _resident((seq, HEAD_DIM)),
                  pl.BlockSpec((1, HEAD_DIM), lambda b, h: (0, h))],
        out_specs=pl.BlockSpec((seq, HEAD_DIM), lambda b, h: (b, h)),
        out_shape=jax.ShapeDtypeStruct((t, nh * HEAD_DIM), BF16),
        compiler_params=_params("parallel", "parallel"),
        name="retention",
    )(log_g, proj, proj, proj, proj, cos2, sin2, gn_w)


def _group_reduce(x, op):
    r, c = x.shape
    return op(x.reshape(r // 8, 8, c), axis=0)


def _moba_kernel(b31_ref, q_ref, k_ref, v_ref, town_ref, tadj_ref, o_ref,
                 vt_ref, s_ref, p_ref):
    blk = MOBA_BLOCK
    nb = k_ref.shape[0] // blk
    b31 = b31_ref[pl.program_id(1)]

    means = []
    for n in range(nb):
        rows = slice(n * blk, (n + 1) * blk)
        means.append(jnp.mean(k_ref[rows, :].astype(F32), axis=0, keepdims=True))
        vt_ref[:, rows] = jnp.transpose(v_ref[rows, :].astype(F32)).astype(BF16)
    km = jnp.concatenate(means, axis=0)
    km_hi = km.astype(BF16)
    km_lo = (km - km_hi.astype(F32)).astype(BF16)
    row = lax.broadcasted_iota(jnp.int32, (nb, blk), 0).astype(F32)

    town = town_ref[...] * LOG2E
    tadj = tadj_ref[...] * LOG2E

    def scores(i):
        slot = i % 2
        q = q_ref[i * blk:(i + 1) * blk, :]
        gate = _dot_nt(km_hi, q) + _dot_nt(km_lo, q)
        valid = row < float(i)
        gm = jnp.where(valid, gate, -jnp.inf)
        sel = jnp.full(gate.shape, NEG, F32)
        for _ in range(min(MOBA_TOPK, nb)):
            mx = jnp.max(gm, axis=0, keepdims=True)
            first = jnp.min(jnp.where(gm == mx, row, float(nb)), axis=0, keepdims=True)
            pick = row == first
            sel = jnp.where(jnp.logical_and(pick, valid), 0.0, sel)
            gm = jnp.where(pick, -jnp.inf, gm)

        qs = (q.astype(F32) * (HEAD_DIM ** -0.5 * LOG2E)).astype(BF16)
        s = _dot_nt(k_ref[0:(i + 1) * blk, :], qs)
        far = sel + b31 * LOG2E
        m8 = None
        for j in range(i + 1):
            rows = slice(j * blk, (j + 1) * blk)
            if j == i:
                bias = town
            elif j == i - 1:
                bias = tadj + sel[j:j + 1, :]
            else:
                bias = far[j:j + 1, :]
            t = s[rows, :] + bias
            s_ref[slot, rows, :] = t
            t8 = _group_reduce(t, jnp.max)
            m8 = t8 if m8 is None else jnp.maximum(m8, t8)
        return jnp.max(m8, axis=0, keepdims=True)

    def attend(i, m):
        slot = i % 2
        l8 = jnp.zeros((8, blk), F32)
        for j in range(i + 1):
            rows = slice(j * blk, (j + 1) * blk)
            p = jnp.exp2(s_ref[slot, rows, :] - m)
            l8 = l8 + _group_reduce(p, jnp.sum)
            p_ref[slot, rows, :] = p.astype(BF16)
        l = jnp.sum(l8, axis=0, keepdims=True)
        n_keys = (i + 1) * blk
        acc = _dot(vt_ref[:, 0:n_keys], p_ref[slot, 0:n_keys, :])
        o_ref[i * blk:(i + 1) * blk, :] = jnp.transpose(acc / l).astype(o_ref.dtype)

    m_next = scores(nb - 1)
    for i in reversed(range(nb)):
        m = m_next
        if i > 0:
            m_next = scores(i - 1)
        attend(i, m)


def _moba(proj, b31, town_t, tadj_t, batch, seq, col0):
    t = proj.shape[0]
    nh = MOBA_HEADS
    blk = MOBA_BLOCK

    def head_spec(group):
        return pl.BlockSpec((seq, HEAD_DIM), lambda b, h: (b, col0 + group * nh + h))

    return pl.pallas_call(
        _moba_kernel,
        grid=(batch, nh),
        in_specs=[pl.BlockSpec(memory_space=pltpu.SMEM),
                  head_spec(0), head_spec(1), head_spec(2),
                  pl.BlockSpec((None, blk, blk), lambda b, h: (h, 0, 0)),
                  pl.BlockSpec((None, blk, blk), lambda b, h: (h, 0, 0))],
        out_specs=pl.BlockSpec((seq, HEAD_DIM), lambda b, h: (b, h)),
        out_shape=jax.ShapeDtypeStruct((t, nh * HEAD_DIM), BF16),
        scratch_shapes=[pltpu.VMEM((HEAD_DIM, seq), BF16),
                        pltpu.VMEM((2, seq, blk), F32),
                        pltpu.VMEM((2, seq, blk), BF16)],
        compiler_params=_params("parallel", "parallel"),
        name="moba",
    )(b31, proj, proj, proj, town_t, tadj_t)


def _fox_gate_kernel(fl_ref, bf_ref, o_ref):
    x = fl_ref[...] + bf_ref[...]
    y = jnp.minimum(x, 0.0) - jnp.log1p(jnp.exp(-jnp.abs(x)))
    n = y.shape[1]
    lane = lax.broadcasted_iota(jnp.int32, y.shape, 1)
    sh = 1
    while sh < n:
        y = y + jnp.where(lane >= sh, pltpu.roll(y, sh, 1), 0.0)
        sh *= 2
    o_ref[...] = y


def _fox_gate(fl_t, b_forget, batch, seq):
    nh = fl_t.shape[0]
    return pl.pallas_call(
        _fox_gate_kernel,
        grid=(batch,),
        in_specs=[pl.BlockSpec((nh, seq), lambda b: (0, b)), _resident((nh, 1))],
        out_specs=pl.BlockSpec((nh, seq), lambda b: (0, b)),
        out_shape=jax.ShapeDtypeStruct((nh, batch * seq), F32),
        compiler_params=_params("parallel"),
        name="fox_gate",
    )(fl_t, b_forget)


def _fox_kernel(q_ref, k_ref, v_ref, fq_ref, fk_ref, o_ref):
    tq = FOX_TILE
    i = pl.program_id(2)
    qs = (q_ref[...].astype(F32) * (HEAD_DIM ** -0.5)).astype(BF16)
    c = jnp.max(fq_ref[...], axis=1, keepdims=True)

    def tile(carry, j, causal):
        m, l, acc = carry
        sl = pl.ds(pl.multiple_of(j * tq, tq), tq)
        s = _dot_nt(qs, k_ref[sl, :]) + (c - fk_ref[j])
        if causal:
            r = lax.broadcasted_iota(jnp.int32, s.shape, 0)
            cc = lax.broadcasted_iota(jnp.int32, s.shape, 1)
            s = jnp.where(cc <= r, s, NEG)
        m_new = jnp.maximum(m, jnp.max(s, axis=1, keepdims=True))
        alpha = jnp.exp(m - m_new)
        p = jnp.exp(s - m_new)
        l = alpha * l + jnp.sum(p, axis=1, keepdims=True)
        acc = alpha * acc + _dot(p.astype(BF16), v_ref[sl, :])
        return m_new, l, acc

    init = (jnp.full((tq, 1), NEG, F32), jnp.zeros((tq, 1), F32),
            jnp.zeros((tq, HEAD_DIM), F32))
    carry = tile(init, i, True)
    m, l, acc = lax.fori_loop(0, i, lambda j, cr: tile(cr, j, False), carry)
    o_ref[...] = (acc / l).astype(o_ref.dtype)


def _fox(qkv, f_tiles, batch, seq):
    t = qkv.shape[0]
    nh = FOX_HEADS
    tq = FOX_TILE
    nq = seq // tq
    return pl.pallas_call(
        _fox_kernel,
        grid=(batch, nh, nq),
        in_specs=[pl.BlockSpec((tq, HEAD_DIM), lambda b, h, i: (b * nq + i, h)),
                  pl.BlockSpec((seq, HEAD_DIM), lambda b, h, i: (b, nh + h)),
                  pl.BlockSpec((seq, HEAD_DIM), lambda b, h, i: (b, 2 * nh + h)),
                  pl.BlockSpec((None, None, None, 1, tq), lambda b, h, i: (b, h, i, 0, 0)),
                  pl.BlockSpec((None, None, nq, 1, tq), lambda b, h, i: (b, h, 0, 0, 0))],
        out_specs=pl.BlockSpec((tq, HEAD_DIM), lambda b, h, i: (b * nq + i, h)),
        out_shape=jax.ShapeDtypeStruct((t, nh * HEAD_DIM), BF16),
        compiler_params=_params("parallel", "parallel", "arbitrary"),
        name="fox",
    )(qkv, qkv, qkv, f_tiles, f_tiles)


def _rotary_tables(seq):
    inv_freq = jnp.power(ROPE_BASE, -jnp.arange(0, HEAD_DIM, 2, dtype=F32) / HEAD_DIM)
    ang = jnp.arange(seq, dtype=F32)[:, None] * inv_freq[None, :]
    cos, sin = jnp.cos(ang), jnp.sin(ang)
    return jnp.concatenate([cos, cos], axis=-1), jnp.concatenate([-sin, sin], axis=-1)


def _t5_bucket(rel):
    n = jnp.maximum(rel, 0)
    max_exact = N_REL_BUCKETS // 2
    nf = jnp.maximum(n, 1).astype(F32)
    large = max_exact + (jnp.log(nf / max_exact) / math.log(REL_MAX_DISTANCE / max_exact)
                         * (N_REL_BUCKETS - max_exact)).astype(jnp.int32)
    large = jnp.minimum(large, N_REL_BUCKETS - 1)
    return jnp.where(n < max_exact, n, large)


def _moba_bias_tables(rel_bias):
    blk = MOBA_BLOCK
    bias_t = rel_bias.astype(F32).T
    kk = jnp.arange(blk)[:, None]
    qq = jnp.arange(blk)[None, :]

    def lookup(rel):
        hot = _t5_bucket(rel)[..., None] == jnp.arange(N_REL_BUCKETS)
        return jnp.sum(jnp.where(hot[None], bias_t[:, None, None, :], 0.0), axis=-1)

    own = jnp.where(qq >= kk, lookup(qq - kk), NEG)
    adj = lookup(blk + qq - kk)
    far = bias_t[:, _t5_bucket(jnp.array(2 * blk, jnp.int32))]
    return far, own, adj


def kernel(x, rel_bias, norm_mix_pre, norm_mix_post, norm_ffn_pre, norm_ffn_post,
           even_w_in, even_gn, even_w_out, odd_w_in, odd_b_forget, odd_w_out,
           ffn_w_in, ffn_w_out):
    batch, seq, d = x.shape
    depth = norm_mix_pre.shape[0]
    hidden = ffn_w_out.shape[1]
    ret_w = RET_HEADS * HEAD_DIM
    fox_w = FOX_HEADS * HEAD_DIM
    xt = x.reshape(batch * seq, d)

    cos2, sin2 = _rotary_tables(seq)
    log_g = jnp.log1p(-jnp.power(2.0, -5.0 - jnp.arange(RET_HEADS, dtype=F32)))
    b31, town_t, tadj_t = _moba_bias_tables(rel_bias)

    def row(v):
        return v.reshape(1, -1).astype(F32)

    for layer in range(depth):
        i = layer // 2
        if layer % 2 == 0:
            proj = _norm_proj(xt, row(norm_mix_pre[layer]), even_w_in[i].astype(BF16))
            ret = _retention(proj, log_g, cos2, sin2, row(even_gn[i]), batch, seq)
            mob = _moba(proj, b31, town_t, tadj_t, batch, seq, 4 * RET_HEADS)
            w_out = even_w_out[i].astype(BF16)
            xt = _out_proj([ret, mob], [w_out[:ret_w], w_out[ret_w:]],
                           row(norm_mix_post[layer]), xt)
        else:
            w_in = odd_w_in[i]
            qkv, fl_t = _norm_proj_gate(xt, row(norm_mix_pre[layer]),
                                        w_in[:, :3 * fox_w].astype(BF16),
                                        w_in[:, 3 * fox_w:].T.astype(F32))
            f = _fox_gate(fl_t, odd_b_forget[i].reshape(-1, 1).astype(F32), batch, seq)
            f_tiles = (f.reshape(FOX_HEADS, batch, seq // FOX_TILE, 1, FOX_TILE)
                       .transpose(1, 0, 2, 3, 4))
            o = _fox(qkv, f_tiles, batch, seq)
            xt = _out_proj([o], [odd_w_out[i].astype(BF16)], row(norm_mix_post[layer]), xt)
        w_ffn = ffn_w_in[layer].astype(BF16)
        xt = _ffn(xt, row(norm_ffn_pre[layer]), w_ffn[:, :hidden], w_ffn[:, hidden:],
                  ffn_w_out[layer].astype(BF16), row(norm_ffn_post[layer]))
    return xt.reshape(batch, seq, d)
```

```python
import functools
import math

import jax
import jax.numpy as jnp
from jax import lax
from jax.experimental import pallas as pl
from jax.experimental.pallas import tpu as pltpu

F32 = jnp.float32
BF16 = jnp.bfloat16

HEAD_DIM = 128
RET_HEADS = 4
MOBA_HEADS = 4
FOX_HEADS = 8
RET_CHUNK = 128
MOBA_BLOCK = 256
MOBA_TOPK = 3
N_REL_BUCKETS = 32
REL_MAX_DISTANCE = 128
ROPE_BASE = 10000.0
RMS_EPS = 1e-6
GN_EPS = 1e-5
NEG = -1e30
LOG2E = math.log2(math.e)

V7X_VMEM_BYTES = 64 * 1024 * 1024
VMEM_LIMIT = V7X_VMEM_BYTES - 8 * 1024 * 1024

ROW_TILE = 512
COL_CHUNK = 512
FFN_CHUNK = 256
FOX_BLOCK = 256


def _params(*sem):
    return pltpu.CompilerParams(dimension_semantics=sem, vmem_limit_bytes=VMEM_LIMIT)


def _resident(shape):
    nd = len(shape)
    return pl.BlockSpec(shape, lambda *_: (0,) * nd, pipeline_mode=pl.Buffered(1))


def _rms(x, g):
    return x * lax.rsqrt(jnp.mean(x * x, axis=-1, keepdims=True) + RMS_EPS) * g


def _dot(a, b):
    return jnp.dot(a, b, preferred_element_type=F32)


def _dot_nt(a, b):
    return lax.dot_general(a, b, (((1,), (1,)), ((), ())), preferred_element_type=F32)


def _norm_proj_kernel(x_ref, g_ref, w_ref, o_ref):
    hn = _rms(x_ref[...], g_ref[...]).astype(BF16)
    for c in range(o_ref.shape[1] // COL_CHUNK):
        sl = slice(c * COL_CHUNK, (c + 1) * COL_CHUNK)
        o_ref[:, sl] = _dot(hn, w_ref[:, sl]).astype(o_ref.dtype)


def _norm_proj(x, g, w):
    t, d = x.shape
    n = w.shape[1]
    return pl.pallas_call(
        _norm_proj_kernel,
        grid=(t // ROW_TILE,),
        in_specs=[pl.BlockSpec((ROW_TILE, d), lambda i: (i, 0)),
                  _resident((1, d)), _resident((d, n))],
        out_specs=pl.BlockSpec((ROW_TILE, n), lambda i: (i, 0)),
        out_shape=jax.ShapeDtypeStruct((t, n), BF16),
        compiler_params=_params("parallel"),
        name="norm_proj",
    )(x, g, w)


def _norm_proj_gate_kernel(x_ref, g_ref, w_ref, o_ref, f_ref):
    hn = _rms(x_ref[...], g_ref[...]).astype(BF16)
    n = o_ref.shape[1]
    for c in range(n // COL_CHUNK):
        sl = slice(c * COL_CHUNK, (c + 1) * COL_CHUNK)
        o_ref[:, sl] = _dot(hn, w_ref[:, sl]).astype(o_ref.dtype)
    fl = _dot(hn, w_ref[:, n:])
    f_ref[...] = jnp.transpose(fl)[0:f_ref.shape[0], :]


def _norm_proj_gate(x, g, w, nh):
    t, d = x.shape
    n = w.shape[1] - HEAD_DIM
    return pl.pallas_call(
        _norm_proj_gate_kernel,
        grid=(t // ROW_TILE,),
        in_specs=[pl.BlockSpec((ROW_TILE, d), lambda i: (i, 0)),
                  _resident((1, d)), _resident(w.shape)],
        out_specs=[pl.BlockSpec((ROW_TILE, n), lambda i: (i, 0)),
                   pl.BlockSpec((nh, ROW_TILE), lambda i: (0, i))],
        out_shape=[jax.ShapeDtypeStruct((t, n), BF16),
                   jax.ShapeDtypeStruct((nh, t), F32)],
        compiler_params=_params("parallel"),
        name="norm_proj_gate",
    )(x, g, w)


def _out_proj_kernel(n_in, *refs):
    a_refs = refs[:n_in]
    w_refs = refs[n_in:2 * n_in]
    g_ref, x_ref, o_ref = refs[2 * n_in:]
    m = _dot(a_refs[0][...], w_refs[0][...])
    for a_ref, w_ref in zip(a_refs[1:], w_refs[1:]):
        m = m + _dot(a_ref[...], w_ref[...])
    o_ref[...] = x_ref[...] + _rms(m, g_ref[...])


def _out_proj(acts, weights, g, x):
    t, d = x.shape
    n_in = len(acts)
    in_specs = [pl.BlockSpec((ROW_TILE, a.shape[1]), lambda i: (i, 0)) for a in acts]
    in_specs += [_resident(w.shape) for w in weights]
    in_specs += [_resident((1, d)), pl.BlockSpec((ROW_TILE, d), lambda i: (i, 0))]
    return pl.pallas_call(
        functools.partial(_out_proj_kernel, n_in),
        grid=(t // ROW_TILE,),
        in_specs=in_specs,
        out_specs=pl.BlockSpec((ROW_TILE, d), lambda i: (i, 0)),
        out_shape=jax.ShapeDtypeStruct((t, d), F32),
        compiler_params=_params("parallel"),
        name="out_proj",
    )(*acts, *weights, g, x)


def _ffn_kernel(x_ref, gpre_ref, wg_ref, wu_ref, wo_ref, gpost_ref, o_ref):
    x = x_ref[...]
    hn = _rms(x, gpre_ref[...]).astype(BF16)
    hidden = wg_ref.shape[1]
    y = jnp.zeros(x.shape, F32)
    for c in range(hidden // FFN_CHUNK):
        sl = slice(c * FFN_CHUNK, (c + 1) * FFN_CHUNK)
        gate = _dot(hn, wg_ref[:, sl])
        up = _dot(hn, wu_ref[:, sl])
        act = (gate * jax.nn.sigmoid(gate) * up).astype(BF16)
        y = y + _dot(act, wo_ref[sl, :])
    o_ref[...] = x + _rms(y, gpost_ref[...])


def _ffn(x, gpre, wg, wu, wo, gpost):
    t, d = x.shape
    return pl.pallas_call(
        _ffn_kernel,
        grid=(t // ROW_TILE,),
        in_specs=[pl.BlockSpec((ROW_TILE, d), lambda i: (i, 0)),
                  _resident((1, d)), _resident(wg.shape), _resident(wu.shape),
                  _resident(wo.shape), _resident((1, d))],
        out_specs=pl.BlockSpec((ROW_TILE, d), lambda i: (i, 0)),
        out_shape=jax.ShapeDtypeStruct((t, d), F32),
        compiler_params=_params("parallel"),
        name="ffn",
    )(x, gpre, wg, wu, wo, gpost)


def _retention_kernel(lg_ref, q_ref, k_ref, v_ref, g_ref, cos_ref, sin_ref, gn_ref, o_ref):
    c = RET_CHUNK
    seq = q_ref.shape[0]
    lg = lg_ref[pl.program_id(1)]
    row = lax.broadcasted_iota(jnp.int32, (c, c), 0).astype(F32)
    col = lax.broadcasted_iota(jnp.int32, (c, c), 1).astype(F32)
    diff = row - col
    inner = jnp.where(diff >= 0, jnp.exp(lg * jnp.maximum(diff, 0.0)), 0.0)
    idx = lax.broadcasted_iota(jnp.int32, (c, HEAD_DIM), 0).astype(F32)
    q_dec = jnp.exp(lg * (idx + 1.0))
    k_dec = jnp.exp(lg * (c - 1.0 - idx))
    chunk_dec = jnp.exp(lg * jnp.full((1, HEAD_DIM), float(c), F32))
    gn_w = gn_ref[...]
    scale = HEAD_DIM ** -0.5

    def rot(t, cos, sin):
        return t * cos + pltpu.roll(t, HEAD_DIM // 2, 1) * sin

    def step(n, state):
        sl = slice(n * c, (n + 1) * c)
        cos = cos_ref[sl, :]
        sin = sin_ref[sl, :]
        q = rot(q_ref[sl, :].astype(F32), cos, sin) * scale
        k = rot(k_ref[sl, :].astype(F32), cos, sin)
        v = v_ref[sl, :]
        qb = q.astype(BF16)
        sc = _dot_nt(qb, k.astype(BF16)) * inner
        o = _dot(sc.astype(BF16), v) + _dot(qb, state.astype(BF16)) * q_dec
        kd_t = jnp.transpose(k * k_dec).astype(BF16)
        state = state * chunk_dec + _dot(kd_t, v)
        mu = jnp.mean(o, axis=-1, keepdims=True)
        var = jnp.mean(jnp.square(o - mu), axis=-1, keepdims=True)
        y = (o - mu) * lax.rsqrt(var + GN_EPS) * gn_w
        gate = g_ref[sl, :].astype(F32)
        o_ref[sl, :] = (y * (gate * jax.nn.sigmoid(gate))).astype(o_ref.dtype)
        return state

    state = jnp.zeros((HEAD_DIM, HEAD_DIM), F32)
    for n in range(seq // c):
        state = step(n, state)


def _retention(proj, log_g, cos2, sin2, gn_w, batch, seq):
    t = proj.shape[0]
    nh = RET_HEADS

    def head_spec(group):
        return pl.BlockSpec((seq, HEAD_DIM), lambda b, h: (b, group * nh + h))

    return pl.pallas_call(
        _retention_kernel,
        grid=(batch, nh),
        in_specs=[pl.BlockSpec(memory_space=pltpu.SMEM),
                  head_spec(0), head_spec(1), head_spec(2), head_spec(3),
                  _resident((seq, HEAD_DIM)), _resident((seq, HEAD_DIM)),
                  pl.BlockSpec((1, HEAD_DIM), lambda b, h: (0, h))],
        out_specs=pl.BlockSpec((seq, HEAD_DIM), lambda b, h: (b, h)),
        out_shape=jax.ShapeDtypeStruct((t, nh * HEAD_DIM), BF16),
        compiler_params=_params("parallel", "parallel"),
        name="retention",
    )(log_g, proj, proj, proj, proj, cos2, sin2, gn_w)


def _group_reduce(x, op):
    r, c = x.shape
    return op(x.reshape(r // 8, 8, c), axis=0)


def _attend(i, m, blk, s_ref, p_ref, vt_ref, o_ref):
    slot = i % 2
    l8 = jnp.zeros((8, blk), F32)
    for j in range(i + 1):
        rows = slice(j * blk, (j + 1) * blk)
        p = jnp.exp2(s_ref[slot, rows, :] - m)
        l8 = l8 + _group_reduce(p, jnp.sum)
        p_ref[slot, rows, :] = p.astype(BF16)
    l = jnp.sum(l8, axis=0, keepdims=True)
    n_keys = (i + 1) * blk
    acc = _dot(vt_ref[:, 0:n_keys], p_ref[slot, 0:n_keys, :])
    o_ref[i * blk:(i + 1) * blk, :] = jnp.transpose(acc / l).astype(o_ref.dtype)


def _moba_kernel(b31_ref, q_ref, k_ref, v_ref, town_ref, tadj_ref, o_ref,
                 vt_ref, s_ref, p_ref):
    blk = MOBA_BLOCK
    nb = k_ref.shape[0] // blk
    b31 = b31_ref[pl.program_id(1)]

    means = []
    for n in range(nb):
        rows = slice(n * blk, (n + 1) * blk)
        means.append(jnp.mean(k_ref[rows, :].astype(F32), axis=0, keepdims=True))
        vt_ref[:, rows] = jnp.transpose(v_ref[rows, :].astype(F32)).astype(BF16)
    km = jnp.concatenate(means, axis=0)
    km_hi = km.astype(BF16)
    km_lo = (km - km_hi.astype(F32)).astype(BF16)
    row = lax.broadcasted_iota(jnp.int32, (nb, blk), 0).astype(F32)

    town = town_ref[...] * LOG2E
    tadj = tadj_ref[...] * LOG2E

    def scores(i):
        slot = i % 2
        q = q_ref[i * blk:(i + 1) * blk, :]
        gate = _dot_nt(km_hi, q) + _dot_nt(km_lo, q)
        valid = row < float(i)
        gm = jnp.where(valid, gate, -jnp.inf)
        sel = jnp.full(gate.shape, NEG, F32)
        for _ in range(min(MOBA_TOPK, nb)):
            mx = jnp.max(gm, axis=0, keepdims=True)
            first = jnp.min(jnp.where(gm == mx, row, float(nb)), axis=0, keepdims=True)
            pick = row == first
            sel = jnp.where(jnp.logical_and(pick, valid), 0.0, sel)
            gm = jnp.where(pick, -jnp.inf, gm)

        qs = (q.astype(F32) * (HEAD_DIM ** -0.5 * LOG2E)).astype(BF16)
        s = _dot_nt(k_ref[0:(i + 1) * blk, :], qs)
        far = sel + b31 * LOG2E
        m8 = None
        for j in range(i + 1):
            rows = slice(j * blk, (j + 1) * blk)
            if j == i:
                bias = town
            elif j == i - 1:
                bias = tadj + sel[j:j + 1, :]
            else:
                bias = far[j:j + 1, :]
            t = s[rows, :] + bias
            s_ref[slot, rows, :] = t
            t8 = _group_reduce(t, jnp.max)
            m8 = t8 if m8 is None else jnp.maximum(m8, t8)
        return jnp.max(m8, axis=0, keepdims=True)

    m_next = scores(nb - 1)
    for i in reversed(range(nb)):
        m = m_next
        if i > 0:
            m_next = scores(i - 1)
        _attend(i, m, blk, s_ref, p_ref, vt_ref, o_ref)


def _moba(proj, b31, town_t, tadj_t, batch, seq, col0):
    t = proj.shape[0]
    nh = MOBA_HEADS
    blk = MOBA_BLOCK

    def head_spec(group):
        return pl.BlockSpec((seq, HEAD_DIM), lambda b, h: (b, col0 + group * nh + h))

    return pl.pallas_call(
        _moba_kernel,
        grid=(batch, nh),
        in_specs=[pl.BlockSpec(memory_space=pltpu.SMEM),
                  head_spec(0), head_spec(1), head_spec(2),
                  pl.BlockSpec((None, blk, blk), lambda b, h: (h, 0, 0)),
                  pl.BlockSpec((None, blk, blk), lambda b, h: (h, 0, 0))],
        out_specs=pl.BlockSpec((seq, HEAD_DIM), lambda b, h: (b, h)),
        out_shape=jax.ShapeDtypeStruct((t, nh * HEAD_DIM), BF16),
        scratch_shapes=[pltpu.VMEM((HEAD_DIM, seq), BF16),
                        pltpu.VMEM((2, seq, blk), F32),
                        pltpu.VMEM((2, seq, blk), BF16)],
        compiler_params=_params("parallel", "parallel"),
        name="moba",
    )(b31, proj, proj, proj, town_t, tadj_t)


def _fox_gate_kernel(fl_ref, bf_ref, o_ref):
    x = fl_ref[...] + bf_ref[...]
    y = jnp.minimum(x, 0.0) - jnp.log1p(jnp.exp(-jnp.abs(x)))
    n = y.shape[1]
    lane = lax.broadcasted_iota(jnp.int32, y.shape, 1)
    sh = 1
    while sh < n:
        y = y + jnp.where(lane >= sh, pltpu.roll(y, sh, 1), 0.0)
        sh *= 2
    o_ref[...] = y


def _fox_gate(fl_t, b_forget, batch, seq):
    nh = fl_t.shape[0]
    return pl.pallas_call(
        _fox_gate_kernel,
        grid=(batch,),
        in_specs=[pl.BlockSpec((nh, seq), lambda b: (0, b)), _resident((nh, 1))],
        out_specs=pl.BlockSpec((nh, seq), lambda b: (0, b)),
        out_shape=jax.ShapeDtypeStruct((nh, batch * seq), F32),
        compiler_params=_params("parallel"),
        name="fox_gate",
    )(fl_t, b_forget)


def _fox_kernel(q_ref, k_ref, v_ref, f_ref, o_ref, ka_ref, vt_ref, s_ref, p_ref):
    blk = FOX_BLOCK
    nb = k_ref.shape[0] // blk
    d = HEAD_DIM
    frow = f_ref[pl.ds(pl.program_id(1), 1), :] * (-LOG2E)
    lane = lax.broadcasted_iota(jnp.int32, (blk, d), 1)
    for n in range(nb):
        rows = slice(n * blk, (n + 1) * blk)
        fcol = jnp.transpose(jnp.broadcast_to(frow[:, rows], (d, blk)))
        hi = fcol.astype(BF16).astype(F32)
        mid = (fcol - hi).astype(BF16).astype(F32)
        lo = fcol - hi - mid
        aug = jnp.where(lane == 0, hi, jnp.where(lane == 1, mid, jnp.where(lane == 2, lo, 0.0)))
        ka_ref[rows, 0:d] = k_ref[rows, :]
        ka_ref[rows, d:2 * d] = aug.astype(BF16)
        vt_ref[:, rows] = jnp.transpose(v_ref[rows, :].astype(F32)).astype(BF16)
    ones = jnp.where(lane < 3, 1.0, 0.0).astype(BF16)
    krow = lax.broadcasted_iota(jnp.int32, (blk, blk), 0)
    qcol = lax.broadcasted_iota(jnp.int32, (blk, blk), 1)
    causal = jnp.where(qcol >= krow, 0.0, NEG)

    def scores(i):
        slot = i % 2
        q = q_ref[i * blk:(i + 1) * blk, :].astype(F32)
        qa = jnp.concatenate([(q * (d ** -0.5 * LOG2E)).astype(BF16), ones], axis=1)
        s = _dot_nt(ka_ref[0:(i + 1) * blk, :], qa)
        m8 = None
        for j in range(i + 1):
            rows = slice(j * blk, (j + 1) * blk)
            t = s[rows, :] + causal if j == i else s[rows, :]
            s_ref[slot, rows, :] = t
            t8 = _group_reduce(t, jnp.max)
            m8 = t8 if m8 is None else jnp.maximum(m8, t8)
        return jnp.max(m8, axis=0, keepdims=True)

    m_next = scores(nb - 1)
    for i in reversed(range(nb)):
        m = m_next
        if i > 0:
            m_next = scores(i - 1)
        _attend(i, m, blk, s_ref, p_ref, vt_ref, o_ref)


def _fox(qkv, f, batch, seq):
    t = qkv.shape[0]
    nh = FOX_HEADS
    blk = FOX_BLOCK

    def head_spec(group):
        return pl.BlockSpec((seq, HEAD_DIM), lambda b, h: (b, group * nh + h))

    return pl.pallas_call(
        _fox_kernel,
        grid=(batch, nh),
        in_specs=[head_spec(0), head_spec(1), head_spec(2),
                  pl.BlockSpec((nh, seq), lambda b, h: (0, b))],
        out_specs=pl.BlockSpec((seq, HEAD_DIM), lambda b, h: (b, h)),
        out_shape=jax.ShapeDtypeStruct((t, nh * HEAD_DIM), BF16),
        scratch_shapes=[pltpu.VMEM((seq, 2 * HEAD_DIM), BF16),
                        pltpu.VMEM((HEAD_DIM, seq), BF16),
                        pltpu.VMEM((2, seq, blk), F32),
                        pltpu.VMEM((2, seq, blk), BF16)],
        compiler_params=_params("parallel", "parallel"),
        name="fox",
    )(qkv, qkv, qkv, f)


def _rotary_tables(seq):
    inv_freq = jnp.power(ROPE_BASE, -jnp.arange(0, HEAD_DIM, 2, dtype=F32) / HEAD_DIM)
    ang = jnp.arange(seq, dtype=F32)[:, None] * inv_freq[None, :]
    cos, sin = jnp.cos(ang), jnp.sin(ang)
    return jnp.concatenate([cos, cos], axis=-1), jnp.concatenate([-sin, sin], axis=-1)


def _t5_bucket(rel):
    n = jnp.maximum(rel, 0)
    max_exact = N_REL_BUCKETS // 2
    nf = jnp.maximum(n, 1).astype(F32)
    large = max_exact + (jnp.log(nf / max_exact) / math.log(REL_MAX_DISTANCE / max_exact)
                         * (N_REL_BUCKETS - max_exact)).astype(jnp.int32)
    large = jnp.minimum(large, N_REL_BUCKETS - 1)
    return jnp.where(n < max_exact, n, large)


def _moba_bias_tables(rel_bias):
    blk = MOBA_BLOCK
    bias_t = rel_bias.astype(F32).T
    kk = jnp.arange(blk)[:, None]
    qq = jnp.arange(blk)[None, :]

    def lookup(rel):
        hot = _t5_bucket(rel)[..., None] == jnp.arange(N_REL_BUCKETS)
        return jnp.sum(jnp.where(hot[None], bias_t[:, None, None, :], 0.0), axis=-1)

    own = jnp.where(qq >= kk, lookup(qq - kk), NEG)
    adj = lookup(blk + qq - kk)
    far = bias_t[:, _t5_bucket(jnp.array(2 * blk, jnp.int32))]
    return far, own, adj


def kernel(x, rel_bias, norm_mix_pre, norm_mix_post, norm_ffn_pre, norm_ffn_post,
           even_w_in, even_gn, even_w_out, odd_w_in, odd_b_forget, odd_w_out,
           ffn_w_in, ffn_w_out):
    batch, seq, d = x.shape
    depth = norm_mix_pre.shape[0]
    hidden = ffn_w_out.shape[1]
    ret_w = RET_HEADS * HEAD_DIM
    fox_w = FOX_HEADS * HEAD_DIM
    xt = x.reshape(batch * seq, d)

    cos2, sin2 = _rotary_tables(seq)
    log_g = jnp.log1p(-jnp.power(2.0, -5.0 - jnp.arange(RET_HEADS, dtype=F32)))
    b31, town_t, tadj_t = _moba_bias_tables(rel_bias)

    def row(v):
        return v.reshape(1, -1).astype(F32)

    for layer in range(depth):
        i = layer // 2
        if layer % 2 == 0:
            proj = _norm_proj(xt, row(norm_mix_pre[layer]), even_w_in[i].astype(BF16))
            ret = _retention(proj, log_g, cos2, sin2, row(even_gn[i]), batch, seq)
            mob = _moba(proj, b31, town_t, tadj_t, batch, seq, 4 * RET_HEADS)
            w_out = even_w_out[i].astype(BF16)
            xt = _out_proj([ret, mob], [w_out[:ret_w], w_out[ret_w:]],
                           row(norm_mix_post[layer]), xt)
        else:
            w_in = jnp.pad(odd_w_in[i], ((0, 0), (0, HEAD_DIM - FOX_HEADS))).astype(BF16)
            qkv, fl_t = _norm_proj_gate(xt, row(norm_mix_pre[layer]), w_in, FOX_HEADS)
            f = _fox_gate(fl_t, odd_b_forget[i].reshape(-1, 1).astype(F32), batch, seq)
            o = _fox(qkv, f, batch, seq)
            xt = _out_proj([o], [odd_w_out[i].astype(BF16)], row(norm_mix_post[layer]), xt)
        w_ffn = ffn_w_in[layer].astype(BF16)
        xt = _ffn(xt, row(norm_ffn_pre[layer]), w_ffn[:, :hidden], w_ffn[:, hidden:],
                  ffn_w_out[layer].astype(BF16), row(norm_ffn_post[layer]))
    return xt.reshape(batch, seq, d)
```

```python
import functools
import math

import jax
import jax.numpy as jnp
from jax import lax
from jax.experimental import pallas as pl
from jax.experimental.pallas import tpu as pltpu

F32 = jnp.float32
BF16 = jnp.bfloat16

HEAD_DIM = 128
RET_HEADS = 4
MOBA_HEADS = 4
FOX_HEADS = 8
RET_CHUNK = 128
MOBA_BLOCK = 256
MOBA_TOPK = 3
N_REL_BUCKETS = 32
REL_MAX_DISTANCE = 128
ROPE_BASE = 10000.0
RMS_EPS = 1e-6
GN_EPS = 1e-5
NEG = -1e30
LOG2E = math.log2(math.e)

V7X_VMEM_BYTES = 64 * 1024 * 1024
VMEM_LIMIT = V7X_VMEM_BYTES - 8 * 1024 * 1024

ROW_TILE = 512
COL_CHUNK = 512
FFN_CHUNK = 256
FOX_BLOCK = 256
ATT_CHUNK = 4
BF16_SUBLANES = 16


def _params(*sem):
    return pltpu.CompilerParams(dimension_semantics=sem, vmem_limit_bytes=VMEM_LIMIT)


def _resident(shape):
    nd = len(shape)
    return pl.BlockSpec(shape, lambda *_: (0,) * nd, pipeline_mode=pl.Buffered(1))


def _rms(x, g):
    return x * lax.rsqrt(jnp.mean(x * x, axis=-1, keepdims=True) + RMS_EPS) * g


def _dot(a, b):
    return jnp.dot(a, b, preferred_element_type=F32)


def _dot_nt(a, b):
    return lax.dot_general(a, b, (((1,), (1,)), ((), ())), preferred_element_type=F32)


def _norm_proj_kernel(x_ref, g_ref, w_ref, o_ref):
    hn = _rms(x_ref[...], g_ref[...]).astype(BF16)
    for c in range(o_ref.shape[1] // COL_CHUNK):
        sl = slice(c * COL_CHUNK, (c + 1) * COL_CHUNK)
        o_ref[:, sl] = _dot(hn, w_ref[:, sl]).astype(o_ref.dtype)


def _norm_proj(x, g, w):
    t, d = x.shape
    n = w.shape[1]
    return pl.pallas_call(
        _norm_proj_kernel,
        grid=(t // ROW_TILE,),
        in_specs=[pl.BlockSpec((ROW_TILE, d), lambda i: (i, 0)),
                  _resident((1, d)), _resident((d, n))],
        out_specs=pl.BlockSpec((ROW_TILE, n), lambda i: (i, 0)),
        out_shape=jax.ShapeDtypeStruct((t, n), BF16),
        compiler_params=_params("parallel"),
        name="norm_proj",
    )(x, g, w)


def _norm_proj_gate_kernel(x_ref, g_ref, w_ref, o_ref, f_ref):
    hn = _rms(x_ref[...], g_ref[...]).astype(BF16)
    n = o_ref.shape[1]
    for c in range(n // COL_CHUNK):
        sl = slice(c * COL_CHUNK, (c + 1) * COL_CHUNK)
        o_ref[:, sl] = _dot(hn, w_ref[:, sl]).astype(o_ref.dtype)
    fl = _dot(hn, w_ref[:, n:])
    f_ref[...] = jnp.transpose(fl)[0:f_ref.shape[0], :]


def _norm_proj_gate(x, g, w, nh):
    t, d = x.shape
    n = w.shape[1] - HEAD_DIM
    return pl.pallas_call(
        _norm_proj_gate_kernel,
        grid=(t // ROW_TILE,),
        in_specs=[pl.BlockSpec((ROW_TILE, d), lambda i: (i, 0)),
                  _resident((1, d)), _resident(w.shape)],
        out_specs=[pl.BlockSpec((ROW_TILE, n), lambda i: (i, 0)),
                   pl.BlockSpec((nh, ROW_TILE), lambda i: (0, i))],
        out_shape=[jax.ShapeDtypeStruct((t, n), BF16),
                   jax.ShapeDtypeStruct((nh, t), F32)],
        compiler_params=_params("parallel"),
        name="norm_proj_gate",
    )(x, g, w)


def _out_proj_kernel(n_in, *refs):
    a_refs = refs[:n_in]
    w_refs = refs[n_in:2 * n_in]
    g_ref, x_ref, o_ref = refs[2 * n_in:]
    m = _dot(a_refs[0][...], w_refs[0][...])
    for a_ref, w_ref in zip(a_refs[1:], w_refs[1:]):
        m = m + _dot(a_ref[...], w_ref[...])
    o_ref[...] = x_ref[...] + _rms(m, g_ref[...])


def _out_proj(acts, weights, g, x):
    t, d = x.shape
    n_in = len(acts)
    in_specs = [pl.BlockSpec((ROW_TILE, a.shape[1]), lambda i: (i, 0)) for a in acts]
    in_specs += [_resident(w.shape) for w in weights]
    in_specs += [_resident((1, d)), pl.BlockSpec((ROW_TILE, d), lambda i: (i, 0))]
    return pl.pallas_call(
        functools.partial(_out_proj_kernel, n_in),
        grid=(t // ROW_TILE,),
        in_specs=in_specs,
        out_specs=pl.BlockSpec((ROW_TILE, d), lambda i: (i, 0)),
        out_shape=jax.ShapeDtypeStruct((t, d), F32),
        compiler_params=_params("parallel"),
        name="out_proj",
    )(*acts, *weights, g, x)


def _ffn_kernel(x_ref, gpre_ref, wg_ref, wu_ref, wo_ref, gpost_ref, o_ref):
    x = x_ref[...]
    hn = _rms(x, gpre_ref[...]).astype(BF16)
    hidden = wg_ref.shape[1]
    y = jnp.zeros(x.shape, F32)
    for c in range(hidden // FFN_CHUNK):
        sl = slice(c * FFN_CHUNK, (c + 1) * FFN_CHUNK)
        gate = _dot(hn, wg_ref[:, sl])
        up = _dot(hn, wu_ref[:, sl])
        act = (gate * jax.nn.sigmoid(gate) * up).astype(BF16)
        y = y + _dot(act, wo_ref[sl, :])
    o_ref[...] = x + _rms(y, gpost_ref[...])


def _ffn(x, gpre, wg, wu, wo, gpost):
    t, d = x.shape
    return pl.pallas_call(
        _ffn_kernel,
        grid=(t // ROW_TILE,),
        in_specs=[pl.BlockSpec((ROW_TILE, d), lambda i: (i, 0)),
                  _resident((1, d)), _resident(wg.shape), _resident(wu.shape),
                  _resident(wo.shape), _resident((1, d))],
        out_specs=pl.BlockSpec((ROW_TILE, d), lambda i: (i, 0)),
        out_shape=jax.ShapeDtypeStruct((t, d), F32),
        compiler_params=_params("parallel"),
        name="ffn",
    )(x, gpre, wg, wu, wo, gpost)


def _retention_kernel(lg_ref, q_ref, k_ref, v_ref, g_ref, cos_ref, sin_ref, gn_ref, o_ref):
    c = RET_CHUNK
    seq = q_ref.shape[0]
    lg = lg_ref[pl.program_id(1)]
    row = lax.broadcasted_iota(jnp.int32, (c, c), 0).astype(F32)
    col = lax.broadcasted_iota(jnp.int32, (c, c), 1).astype(F32)
    diff = row - col
    inner = jnp.where(diff >= 0, jnp.exp(lg * jnp.maximum(diff, 0.0)), 0.0)
    idx = lax.broadcasted_iota(jnp.int32, (c, HEAD_DIM), 0).astype(F32)
    q_dec = jnp.exp(lg * (idx + 1.0))
    k_dec = jnp.exp(lg * (c - 1.0 - idx))
    chunk_dec = jnp.exp(lg * jnp.full((1, HEAD_DIM), float(c), F32))
    gn_w = gn_ref[...]
    scale = HEAD_DIM ** -0.5

    def rot(t, cos, sin):
        return t * cos + pltpu.roll(t, HEAD_DIM // 2, 1) * sin

    def step(n, state):
        sl = slice(n * c, (n + 1) * c)
        cos = cos_ref[sl, :]
        sin = sin_ref[sl, :]
        q = rot(q_ref[sl, :].astype(F32), cos, sin) * scale
        k = rot(k_ref[sl, :].astype(F32), cos, sin)
        v = v_ref[sl, :]
        qb = q.astype(BF16)
        sc = _dot_nt(qb, k.astype(BF16)) * inner
        o = _dot(sc.astype(BF16), v) + _dot(qb, state.astype(BF16)) * q_dec
        kd_t = jnp.transpose(k * k_dec).astype(BF16)
        state = state * chunk_dec + _dot(kd_t, v)
        mu = jnp.mean(o, axis=-1, keepdims=True)
        var = jnp.mean(jnp.square(o - mu), axis=-1, keepdims=True)
        y = (o - mu) * lax.rsqrt(var + GN_EPS) * gn_w
        gate = g_ref[sl, :].astype(F32)
        o_ref[sl, :] = (y * (gate * jax.nn.sigmoid(gate))).astype(o_ref.dtype)
        return state

    state = jnp.zeros((HEAD_DIM, HEAD_DIM), F32)
    for n in range(seq // c):
        state = step(n, state)


def _retention(proj, log_g, cos2, sin2, gn_w, batch, seq):
    t = proj.shape[0]
    nh = RET_HEADS

    def head_spec(group):
        return pl.BlockSpec((seq, HEAD_DIM), lambda b, h: (b, group * nh + h))

    return pl.pallas_call(
        _retention_kernel,
        grid=(batch, nh),
        in_specs=[pl.BlockSpec(memory_space=pltpu.SMEM),
                  head_spec(0), head_spec(1), head_spec(2), head_spec(3),
                  _resident((seq, HEAD_DIM)), _resident((seq, HEAD_DIM)),
                  pl.BlockSpec((1, HEAD_DIM), lambda b, h: (0, h))],
        out_specs=pl.BlockSpec((seq, HEAD_DIM), lambda b, h: (b, h)),
        out_shape=jax.ShapeDtypeStruct((t, nh * HEAD_DIM), BF16),
        compiler_params=_params("parallel", "parallel"),
        name="retention",
    )(log_g, proj, proj, proj, proj, cos2, sin2, gn_w)


def _group_reduce(x, op):
    r, c = x.shape
    g = r // 8
    if g % 4 == 0:
        return op(op(x.reshape(4, g // 4, 8, c), axis=1), axis=0)
    return op(x.reshape(g, 8, c), axis=0)


def _block_attention(nb, blk, keys_ref, vt_ref, s_ref, o_ref, prep, bias):
    d = HEAD_DIM

    def chunks(i):
        return [(lo, min(lo + ATT_CHUNK, i + 1)) for lo in range(0, i + 1, ATT_CHUNK)]

    def score_chunk(i, lo, hi, qa, aux, m8):
        s = _dot_nt(keys_ref[lo * blk:hi * blk, :], qa)
        for j in range(lo, hi):
            t = s[(j - lo) * blk:(j - lo + 1) * blk, :]
            b = bias(i, j, aux)
            if b is not None:
                t = t + b
            s_ref[i % 2, j * blk:(j + 1) * blk, :] = t
            t8 = _group_reduce(t, jnp.max)
            m8 = t8 if m8 is None else jnp.maximum(m8, t8)
        return m8

    def attend_chunk(i, lo, hi, m, acc):
        ps = [jnp.exp2(s_ref[i % 2, j * blk:(j + 1) * blk, :] - m).astype(BF16)
              for j in range(lo, hi)]
        p = ps[0] if len(ps) == 1 else jnp.concatenate(ps, axis=0)
        part = _dot(vt_ref[:, lo * blk:hi * blk], p)
        return part if acc is None else acc + part

    qa, aux = prep(nb - 1)
    m8 = None
    for lo, hi in chunks(nb - 1):
        m8 = score_chunk(nb - 1, lo, hi, qa, aux, m8)
    for i in reversed(range(nb)):
        m = jnp.max(m8, axis=0, keepdims=True)
        if i > 0:
            qa, aux = prep(i - 1)
        ahead = chunks(i - 1) if i > 0 else []
        here = chunks(i)
        m8 = None
        acc = None
        for c in range(len(here)):
            if c < len(ahead):
                m8 = score_chunk(i - 1, *ahead[c], qa, aux, m8)
            acc = attend_chunk(i, *here[c], m, acc)
        out = acc[0:d, :] / acc[d:d + 1, :]
        o_ref[i * blk:(i + 1) * blk, :] = jnp.transpose(out).astype(o_ref.dtype)


def _fill_values(v_ref, vt_ref, blk):
    d = HEAD_DIM
    seq = v_ref.shape[0]
    for n in range(seq // blk):
        rows = slice(n * blk, (n + 1) * blk)
        vt_ref[0:d, rows] = jnp.transpose(v_ref[rows, :].astype(F32)).astype(BF16)
    first = lax.broadcasted_iota(jnp.int32, (vt_ref.shape[0] - d, seq), 0) == 0
    vt_ref[d:, :] = jnp.where(first, 1.0, 0.0).astype(BF16)


def _moba_kernel(b31_ref, q_ref, k_ref, v_ref, town_ref, tadj_ref, o_ref, vt_ref, s_ref):
    blk = MOBA_BLOCK
    nb = k_ref.shape[0] // blk
    b31 = b31_ref[pl.program_id(1)]

    _fill_values(v_ref, vt_ref, blk)
    means = [jnp.mean(k_ref[n * blk:(n + 1) * blk, :].astype(F32), axis=0, keepdims=True)
             for n in range(nb)]
    km = jnp.concatenate(means, axis=0)
    km_hi = km.astype(BF16)
    km_lo = (km - km_hi.astype(F32)).astype(BF16)
    row = lax.broadcasted_iota(jnp.int32, (nb, blk), 0).astype(F32)
    town = town_ref[...] * LOG2E
    tadj = tadj_ref[...] * LOG2E

    def prep(i):
        q = q_ref[i * blk:(i + 1) * blk, :]
        gate = _dot_nt(km_hi, q) + _dot_nt(km_lo, q)
        valid = row < float(i)
        gm = jnp.where(valid, gate, -jnp.inf)
        sel = jnp.full(gate.shape, NEG, F32)
        for _ in range(min(MOBA_TOPK, nb)):
            mx = jnp.max(gm, axis=0, keepdims=True)
            first = jnp.min(jnp.where(gm == mx, row, float(nb)), axis=0, keepdims=True)
            pick = row == first
            sel = jnp.where(jnp.logical_and(pick, valid), 0.0, sel)
            gm = jnp.where(pick, -jnp.inf, gm)

        qs = (q.astype(F32) * (HEAD_DIM ** -0.5 * LOG2E)).astype(BF16)
        return qs, (sel, sel + b31 * LOG2E)

    def bias(i, j, aux):
        sel, far = aux
        if j == i:
            return town
        if j == i - 1:
            return tadj + sel[j:j + 1, :]
        return far[j:j + 1, :]

    _block_attention(nb, blk, k_ref, vt_ref, s_ref, o_ref, prep, bias)


def _moba(proj, b31, town_t, tadj_t, batch, seq, col0):
    t = proj.shape[0]
    nh = MOBA_HEADS
    blk = MOBA_BLOCK

    def head_spec(group):
        return pl.BlockSpec((seq, HEAD_DIM), lambda b, h: (b, col0 + group * nh + h))

    return pl.pallas_call(
        _moba_kernel,
        grid=(batch, nh),
        in_specs=[pl.BlockSpec(memory_space=pltpu.SMEM),
                  head_spec(0), head_spec(1), head_spec(2),
                  pl.BlockSpec((None, blk, blk), lambda b, h: (h, 0, 0)),
                  pl.BlockSpec((None, blk, blk), lambda b, h: (h, 0, 0))],
        out_specs=pl.BlockSpec((seq, HEAD_DIM), lambda b, h: (b, h)),
        out_shape=jax.ShapeDtypeStruct((t, nh * HEAD_DIM), BF16),
        scratch_shapes=[pltpu.VMEM((HEAD_DIM + BF16_SUBLANES, seq), BF16),
                        pltpu.VMEM((2, seq, blk), F32)],
        compiler_params=_params("parallel", "parallel"),
        name="moba",
    )(b31, proj, proj, proj, town_t, tadj_t)


def _fox_gate_kernel(fl_ref, bf_ref, o_ref):
    x = fl_ref[...] + bf_ref[...]
    y = jnp.minimum(x, 0.0) - jnp.log1p(jnp.exp(-jnp.abs(x)))
    n = y.shape[1]
    lane = lax.broadcasted_iota(jnp.int32, y.shape, 1)
    sh = 1
    while sh < n:
        y = y + jnp.where(lane >= sh, pltpu.roll(y, sh, 1), 0.0)
        sh *= 2
    o_ref[...] = y


def _fox_gate(fl_t, b_forget, batch, seq):
    nh = fl_t.shape[0]
    return pl.pallas_call(
        _fox_gate_kernel,
        grid=(batch,),
        in_specs=[pl.BlockSpec((nh, seq), lambda b: (0, b)), _resident((nh, 1))],
        out_specs=pl.BlockSpec((nh, seq), lambda b: (0, b)),
        out_shape=jax.ShapeDtypeStruct((nh, batch * seq), F32),
        compiler_params=_params("parallel"),
        name="fox_gate",
    )(fl_t, b_forget)


def _fox_kernel(q_ref, k_ref, v_ref, f_ref, o_ref, ka_ref, vt_ref, s_ref):
    blk = FOX_BLOCK
    nb = k_ref.shape[0] // blk
    d = HEAD_DIM
    frow = f_ref[pl.ds(pl.program_id(1), 1), :] * (-LOG2E)
    lane = lax.broadcasted_iota(jnp.int32, (blk, d), 1)
    for n in range(nb):
        rows = slice(n * blk, (n + 1) * blk)
        fcol = jnp.transpose(jnp.broadcast_to(frow[:, rows], (d, blk)))
        hi = fcol.astype(BF16).astype(F32)
        mid = (fcol - hi).astype(BF16).astype(F32)
        lo = fcol - hi - mid
        aug = jnp.where(lane == 0, hi, jnp.where(lane == 1, mid, jnp.where(lane == 2, lo, 0.0)))
        ka_ref[rows, 0:d] = k_ref[rows, :]
        ka_ref[rows, d:2 * d] = aug.astype(BF16)
    _fill_values(v_ref, vt_ref, blk)
    ones = jnp.where(lane < 3, 1.0, 0.0).astype(BF16)
    krow = lax.broadcasted_iota(jnp.int32, (blk, blk), 0)
    qcol = lax.broadcasted_iota(jnp.int32, (blk, blk), 1)
    causal = jnp.where(qcol >= krow, 0.0, NEG)

    def prep(i):
        q = q_ref[i * blk:(i + 1) * blk, :].astype(F32)
        return jnp.concatenate([(q * (d ** -0.5 * LOG2E)).astype(BF16), ones], axis=1), None

    def bias(i, j, aux):
        return causal if j == i else None

    _block_attention(nb, blk, ka_ref, vt_ref, s_ref, o_ref, prep, bias)


def _fox(qkv, f, batch, seq):
    t = qkv.shape[0]
    nh = FOX_HEADS
    blk = FOX_BLOCK

    def head_spec(group):
        return pl.BlockSpec((seq, HEAD_DIM), lambda b, h: (b, group * nh + h))

    return pl.pallas_call(
        _fox_kernel,
        grid=(batch, nh),
        in_specs=[head_spec(0), head_spec(1), head_spec(2),
                  pl.BlockSpec((nh, seq), lambda b, h: (0, b))],
        out_specs=pl.BlockSpec((seq, HEAD_DIM), lambda b, h: (b, h)),
        out_shape=jax.ShapeDtypeStruct((t, nh * HEAD_DIM), BF16),
        scratch_shapes=[pltpu.VMEM((seq, 2 * HEAD_DIM), BF16),
                        pltpu.VMEM((HEAD_DIM + BF16_SUBLANES, seq), BF16),
                        pltpu.VMEM((2, seq, blk), F32)],
        compiler_params=_params("parallel", "parallel"),
        name="fox",
    )(qkv, qkv, qkv, f)


def _rotary_tables(seq):
    inv_freq = jnp.power(ROPE_BASE, -jnp.arange(0, HEAD_DIM, 2, dtype=F32) / HEAD_DIM)
    ang = jnp.arange(seq, dtype=F32)[:, None] * inv_freq[None, :]
    cos, sin = jnp.cos(ang), jnp.sin(ang)
    return jnp.concatenate([cos, cos], axis=-1), jnp.concatenate([-sin, sin], axis=-1)


def _t5_bucket(rel):
    n = jnp.maximum(rel, 0)
    max_exact = N_REL_BUCKETS // 2
    nf = jnp.maximum(n, 1).astype(F32)
    large = max_exact + (jnp.log(nf / max_exact) / math.log(REL_MAX_DISTANCE / max_exact)
                         * (N_REL_BUCKETS - max_exact)).astype(jnp.int32)
    large = jnp.minimum(large, N_REL_BUCKETS - 1)
    return jnp.where(n < max_exact, n, large)


def _moba_bias_tables(rel_bias):
    blk = MOBA_BLOCK
    bias_t = rel_bias.astype(F32).T
    kk = jnp.arange(blk)[:, None]
    qq = jnp.arange(blk)[None, :]

    def lookup(rel):
        hot = _t5_bucket(rel)[..., None] == jnp.arange(N_REL_BUCKETS)
        return jnp.sum(jnp.where(hot[None], bias_t[:, None, None, :], 0.0), axis=-1)

    own = jnp.where(qq >= kk, lookup(qq - kk), NEG)
    adj = lookup(blk + qq - kk)
    far = bias_t[:, _t5_bucket(jnp.array(2 * blk, jnp.int32))]
    return far, own, adj


def kernel(x, rel_bias, norm_mix_pre, norm_mix_post, norm_ffn_pre, norm_ffn_post,
           even_w_in, even_gn, even_w_out, odd_w_in, odd_b_forget, odd_w_out,
           ffn_w_in, ffn_w_out):
    batch, seq, d = x.shape
    depth = norm_mix_pre.shape[0]
    hidden = ffn_w_out.shape[1]
    ret_w = RET_HEADS * HEAD_DIM
    fox_w = FOX_HEADS * HEAD_DIM
    xt = x.reshape(batch * seq, d)

    cos2, sin2 = _rotary_tables(seq)
    log_g = jnp.log1p(-jnp.power(2.0, -5.0 - jnp.arange(RET_HEADS, dtype=F32)))
    b31, town_t, tadj_t = _moba_bias_tables(rel_bias)

    def row(v):
        return v.reshape(1, -1).astype(F32)

    for layer in range(depth):
        i = layer // 2
        if layer % 2 == 0:
            proj = _norm_proj(xt, row(norm_mix_pre[layer]), even_w_in[i].astype(BF16))
            ret = _retention(proj, log_g, cos2, sin2, row(even_gn[i]), batch, seq)
            mob = _moba(proj, b31, town_t, tadj_t, batch, seq, 4 * RET_HEADS)
            w_out = even_w_out[i].astype(BF16)
            xt = _out_proj([ret, mob], [w_out[:ret_w], w_out[ret_w:]],
                           row(norm_mix_post[layer]), xt)
        else:
            w_in = jnp.pad(odd_w_in[i], ((0, 0), (0, HEAD_DIM - FOX_HEADS))).astype(BF16)
            qkv, fl_t = _norm_proj_gate(xt, row(norm_mix_pre[layer]), w_in, FOX_HEADS)
            f = _fox_gate(fl_t, odd_b_forget[i].reshape(-1, 1).astype(F32), batch, seq)
            o = _fox(qkv, f, batch, seq)
            xt = _out_proj([o], [odd_w_out[i].astype(BF16)], row(norm_mix_post[layer]), xt)
        w_ffn = ffn_w_in[layer].astype(BF16)
        xt = _ffn(xt, row(norm_ffn_pre[layer]), w_ffn[:, :hidden], w_ffn[:, hidden:],
                  ffn_w_out[layer].astype(BF16), row(norm_ffn_post[layer]))
    return xt.reshape(batch, seq, d)
```

```python
import functools
import math

import jax
import jax.numpy as jnp
from jax import lax
from jax.experimental import pallas as pl
from jax.experimental.pallas import tpu as pltpu

F32 = jnp.float32
BF16 = jnp.bfloat16

HEAD_DIM = 128
RET_HEADS = 4
MOBA_HEADS = 4
FOX_HEADS = 8
RET_CHUNK = 128
MOBA_BLOCK = 256
MOBA_TOPK = 3
N_REL_BUCKETS = 32
REL_MAX_DISTANCE = 128
ROPE_BASE = 10000.0
RMS_EPS = 1e-6
GN_EPS = 1e-5
NEG = -1e30
LOG2E = math.log2(math.e)

V7X_VMEM_BYTES = 64 * 1024 * 1024
VMEM_LIMIT = V7X_VMEM_BYTES - 8 * 1024 * 1024

ROW_TILE = 512
COL_CHUNK = 512
FFN_CHUNK = 256
FOX_BLOCK = 256
ATT_CHUNK = 4
BF16_SUBLANES = 16


def _params(*sem):
    return pltpu.CompilerParams(dimension_semantics=sem, vmem_limit_bytes=VMEM_LIMIT)


def _resident(shape):
    nd = len(shape)
    return pl.BlockSpec(shape, lambda *_: (0,) * nd, pipeline_mode=pl.Buffered(1))


def _rms(x, g):
    return x * lax.rsqrt(jnp.mean(x * x, axis=-1, keepdims=True) + RMS_EPS) * g


def _dot(a, b):
    return jnp.dot(a, b, preferred_element_type=F32)


def _dot_nt(a, b):
    return lax.dot_general(a, b, (((1,), (1,)), ((), ())), preferred_element_type=F32)


def _norm_proj_kernel(x_ref, g_ref, w_ref, o_ref):
    hn = _rms(x_ref[...], g_ref[...]).astype(BF16)
    for c in range(o_ref.shape[1] // COL_CHUNK):
        sl = slice(c * COL_CHUNK, (c + 1) * COL_CHUNK)
        o_ref[:, sl] = _dot(hn, w_ref[:, sl]).astype(o_ref.dtype)


def _norm_proj(x, g, w):
    t, d = x.shape
    n = w.shape[1]
    return pl.pallas_call(
        _norm_proj_kernel,
        grid=(t // ROW_TILE,),
        in_specs=[pl.BlockSpec((ROW_TILE, d), lambda i: (i, 0)),
                  _resident((1, d)), _resident((d, n))],
        out_specs=pl.BlockSpec((ROW_TILE, n), lambda i: (i, 0)),
        out_shape=jax.ShapeDtypeStruct((t, n), BF16),
        compiler_params=_params("parallel"),
        name="norm_proj",
    )(x, g, w)


def _norm_proj_gate_kernel(x_ref, g_ref, w_ref, o_ref, f_ref):
    hn = _rms(x_ref[...], g_ref[...]).astype(BF16)
    n = o_ref.shape[1]
    for c in range(n // COL_CHUNK):
        sl = slice(c * COL_CHUNK, (c + 1) * COL_CHUNK)
        o_ref[:, sl] = _dot(hn, w_ref[:, sl]).astype(o_ref.dtype)
    fl = _dot(hn, w_ref[:, n:])
    f_ref[...] = jnp.transpose(fl)[0:f_ref.shape[0], :]


def _norm_proj_gate(x, g, w, nh):
    t, d = x.shape
    n = w.shape[1] - HEAD_DIM
    return pl.pallas_call(
        _norm_proj_gate_kernel,
        grid=(t // ROW_TILE,),
        in_specs=[pl.BlockSpec((ROW_TILE, d), lambda i: (i, 0)),
                  _resident((1, d)), _resident(w.shape)],
        out_specs=[pl.BlockSpec((ROW_TILE, n), lambda i: (i, 0)),
                   pl.BlockSpec((nh, ROW_TILE), lambda i: (0, i))],
        out_shape=[jax.ShapeDtypeStruct((t, n), BF16),
                   jax.ShapeDtypeStruct((nh, t), F32)],
        compiler_params=_params("parallel"),
        name="norm_proj_gate",
    )(x, g, w)


def _mix_ffn_kernel(n_in, *refs):
    a_refs = refs[:n_in]
    wmix_ref, gmix_ref, x_ref, gpre_ref, wffn_ref, wo_ref, gpost_ref, o_ref = refs[n_in:]
    m = None
    off = 0
    for a_ref in a_refs:
        width = a_ref.shape[1]
        part = _dot(a_ref[...], wmix_ref[off:off + width, :])
        m = part if m is None else m + part
        off += width
    x = x_ref[...] + _rms(m, gmix_ref[...])
    hn = _rms(x, gpre_ref[...]).astype(BF16)
    hidden = wo_ref.shape[0]
    y = jnp.zeros(x.shape, F32)
    for c in range(hidden // FFN_CHUNK):
        lo = c * FFN_CHUNK
        gate = _dot(hn, wffn_ref[:, lo:lo + FFN_CHUNK])
        up = _dot(hn, wffn_ref[:, hidden + lo:hidden + lo + FFN_CHUNK])
        act = (gate * jax.nn.sigmoid(gate) * up).astype(BF16)
        y = y + _dot(act, wo_ref[lo:lo + FFN_CHUNK, :])
    o_ref[...] = x + _rms(y, gpost_ref[...])


def _mix_ffn(acts, w_mix, g_mix, x, g_pre, w_ffn, w_o, g_post):
    t, d = x.shape
    in_specs = [pl.BlockSpec((ROW_TILE, a.shape[1]), lambda i: (i, 0)) for a in acts]
    in_specs += [_resident(w_mix.shape), _resident((1, d)),
                 pl.BlockSpec((ROW_TILE, d), lambda i: (i, 0)),
                 _resident((1, d)), _resident(w_ffn.shape), _resident(w_o.shape),
                 _resident((1, d))]
    return pl.pallas_call(
        functools.partial(_mix_ffn_kernel, len(acts)),
        grid=(t // ROW_TILE,),
        in_specs=in_specs,
        out_specs=pl.BlockSpec((ROW_TILE, d), lambda i: (i, 0)),
        out_shape=jax.ShapeDtypeStruct((t, d), F32),
        compiler_params=_params("parallel"),
        name="mix_ffn",
    )(*acts, w_mix, g_mix, x, g_pre, w_ffn, w_o, g_post)


def _retention_kernel(lg_ref, q_ref, k_ref, v_ref, g_ref, cos_ref, sin_ref, gn_ref, o_ref):
    c = RET_CHUNK
    seq = q_ref.shape[0]
    lg = lg_ref[pl.program_id(1)]
    row = lax.broadcasted_iota(jnp.int32, (c, c), 0).astype(F32)
    col = lax.broadcasted_iota(jnp.int32, (c, c), 1).astype(F32)
    diff = row - col
    inner = jnp.where(diff >= 0, jnp.exp(lg * jnp.maximum(diff, 0.0)), 0.0)
    idx = lax.broadcasted_iota(jnp.int32, (c, HEAD_DIM), 0).astype(F32)
    q_dec = jnp.exp(lg * (idx + 1.0))
    k_dec = jnp.exp(lg * (c - 1.0 - idx))
    chunk_dec = jnp.exp(lg * jnp.full((1, HEAD_DIM), float(c), F32))
    gn_w = gn_ref[...]
    scale = HEAD_DIM ** -0.5

    def rot(t, cos, sin):
        return t * cos + pltpu.roll(t, HEAD_DIM // 2, 1) * sin

    def step(n, state):
        sl = slice(n * c, (n + 1) * c)
        cos = cos_ref[sl, :]
        sin = sin_ref[sl, :]
        q = rot(q_ref[sl, :].astype(F32), cos, sin) * scale
        k = rot(k_ref[sl, :].astype(F32), cos, sin)
        v = v_ref[sl, :]
        qb = q.astype(BF16)
        sc = _dot_nt(qb, k.astype(BF16)) * inner
        o = _dot(sc.astype(BF16), v) + _dot(qb, state.astype(BF16)) * q_dec
        kd_t = jnp.transpose(k * k_dec).astype(BF16)
        state = state * chunk_dec + _dot(kd_t, v)
        mu = jnp.mean(o, axis=-1, keepdims=True)
        var = jnp.mean(jnp.square(o - mu), axis=-1, keepdims=True)
        y = (o - mu) * lax.rsqrt(var + GN_EPS) * gn_w
        gate = g_ref[sl, :].astype(F32)
        o_ref[sl, :] = (y * (gate * jax.nn.sigmoid(gate))).astype(o_ref.dtype)
        return state

    state = jnp.zeros((HEAD_DIM, HEAD_DIM), F32)
    for n in range(seq // c):
        state = step(n, state)


def _retention(proj, log_g, cos2, sin2, gn_w, batch, seq):
    t = proj.shape[0]
    nh = RET_HEADS

    def head_spec(group):
        return pl.BlockSpec((seq, HEAD_DIM), lambda b, h: (b, group * nh + h))

    return pl.pallas_call(
        _retention_kernel,
        grid=(batch, nh),
        in_specs=[pl.BlockSpec(memory_space=pltpu.SMEM),
                  head_spec(0), head_spec(1), head_spec(2), head_spec(3),
                  _resident((seq, HEAD_DIM)), _resident((seq, HEAD_DIM)),
                  pl.BlockSpec((1, HEAD_DIM), lambda b, h: (0, h))],
        out_specs=pl.BlockSpec((seq, HEAD_DIM), lambda b, h: (b, h)),
        out_shape=jax.ShapeDtypeStruct((t, nh * HEAD_DIM), BF16),
        compiler_params=_params("parallel", "parallel"),
        name="retention",
    )(log_g, proj, proj, proj, proj, cos2, sin2, gn_w)


def _group_reduce(x, op):
    r, c = x.shape
    g = r // 8
    if g % 4 == 0:
        return op(op(x.reshape(4, g // 4, 8, c), axis=1), axis=0)
    return op(x.reshape(g, 8, c), axis=0)


def _block_attention(nb, blk, keys_ref, vt_ref, s_ref, o_ref, prep, bias):
    d = HEAD_DIM

    def chunks(i):
        return [(lo, min(lo + ATT_CHUNK, i + 1)) for lo in range(0, i + 1, ATT_CHUNK)]

    def score_chunk(i, lo, hi, qa, aux, m8):
        s = _dot_nt(keys_ref[lo * blk:hi * blk, :], qa)
        for j in range(lo, hi):
            t = s[(j - lo) * blk:(j - lo + 1) * blk, :]
            b = bias(i, j, aux)
            if b is not None:
                t = t + b
            s_ref[i % 2, j * blk:(j + 1) * blk, :] = t
            t8 = _group_reduce(t, jnp.max)
            m8 = t8 if m8 is None else jnp.maximum(m8, t8)
        return m8

    def attend_chunk(i, lo, hi, m, acc):
        ps = [jnp.exp2(s_ref[i % 2, j * blk:(j + 1) * blk, :] - m).astype(BF16)
              for j in range(lo, hi)]
        p = ps[0] if len(ps) == 1 else jnp.concatenate(ps, axis=0)
        part = _dot(vt_ref[:, lo * blk:hi * blk], p)
        return part if acc is None else acc + part

    qa, aux = prep(nb - 1)
    m8 = None
    for lo, hi in chunks(nb - 1):
        m8 = score_chunk(nb - 1, lo, hi, qa, aux, m8)
    for i in reversed(range(nb)):
        m = jnp.max(m8, axis=0, keepdims=True)
        if i > 0:
            qa, aux = prep(i - 1)
        ahead = chunks(i - 1) if i > 0 else []
        here = chunks(i)
        m8 = None
        acc = None
        for c in range(len(here)):
            if c < len(ahead):
                m8 = score_chunk(i - 1, *ahead[c], qa, aux, m8)
            acc = attend_chunk(i, *here[c], m, acc)
        out = acc[0:d, :] / acc[d:d + 1, :]
        o_ref[i * blk:(i + 1) * blk, :] = jnp.transpose(out).astype(o_ref.dtype)


def _fill_values(v_ref, vt_ref, blk):
    d = HEAD_DIM
    seq = v_ref.shape[0]
    for n in range(seq // blk):
        rows = slice(n * blk, (n + 1) * blk)
        vt_ref[0:d, rows] = jnp.transpose(v_ref[rows, :].astype(F32)).astype(BF16)
    first = lax.broadcasted_iota(jnp.int32, (vt_ref.shape[0] - d, seq), 0) == 0
    vt_ref[d:, :] = jnp.where(first, 1.0, 0.0).astype(BF16)


def _moba_kernel(b31_ref, q_ref, k_ref, v_ref, bvec_ref, o_ref, vt_ref, s_ref):
    blk = MOBA_BLOCK
    nb = k_ref.shape[0] // blk
    b31 = b31_ref[pl.program_id(1)]

    _fill_values(v_ref, vt_ref, blk)
    means = [jnp.mean(k_ref[n * blk:(n + 1) * blk, :].astype(F32), axis=0, keepdims=True)
             for n in range(nb)]
    km = jnp.concatenate(means, axis=0)
    km_hi = km.astype(BF16)
    km_lo = (km - km_hi.astype(F32)).astype(BF16)
    row = lax.broadcasted_iota(jnp.int32, (nb, blk), 0).astype(F32)
    bvec = jnp.broadcast_to(bvec_ref[...] * LOG2E, (blk, 2 * blk))
    toep = pltpu.roll(bvec, 0, 1, stride=1, stride_axis=0)
    krow = lax.broadcasted_iota(jnp.int32, (blk, blk), 0)
    qcol = lax.broadcasted_iota(jnp.int32, (blk, blk), 1)
    town = jnp.where(qcol >= krow, toep[:, 0:blk], NEG)
    tadj = toep[:, blk:2 * blk]

    def prep(i):
        q = q_ref[i * blk:(i + 1) * blk, :]
        gate = _dot_nt(km_hi, q) + _dot_nt(km_lo, q)
        valid = row < float(i)
        gm = jnp.where(valid, gate, -jnp.inf)
        sel = jnp.full(gate.shape, NEG, F32)
        for _ in range(min(MOBA_TOPK, nb)):
            mx = jnp.max(gm, axis=0, keepdims=True)
            first = jnp.min(jnp.where(gm == mx, row, float(nb)), axis=0, keepdims=True)
            pick = row == first
            sel = jnp.where(jnp.logical_and(pick, valid), 0.0, sel)
            gm = jnp.where(pick, -jnp.inf, gm)

        qs = (q.astype(F32) * (HEAD_DIM ** -0.5 * LOG2E)).astype(BF16)
        return qs, (sel, sel + b31 * LOG2E)

    def bias(i, j, aux):
        sel, far = aux
        if j == i:
            return town
        if j == i - 1:
            return tadj + sel[j:j + 1, :]
        return far[j:j + 1, :]

    _block_attention(nb, blk, k_ref, vt_ref, s_ref, o_ref, prep, bias)


def _moba(proj, b31, bvec, batch, seq, col0):
    t = proj.shape[0]
    nh = MOBA_HEADS
    blk = MOBA_BLOCK

    def head_spec(group):
        return pl.BlockSpec((seq, HEAD_DIM), lambda b, h: (b, col0 + group * nh + h))

    return pl.pallas_call(
        _moba_kernel,
        grid=(batch, nh),
        in_specs=[pl.BlockSpec(memory_space=pltpu.SMEM),
                  head_spec(0), head_spec(1), head_spec(2),
                  pl.BlockSpec((None, 1, 2 * blk), lambda b, h: (h, 0, 0))],
        out_specs=pl.BlockSpec((seq, HEAD_DIM), lambda b, h: (b, h)),
        out_shape=jax.ShapeDtypeStruct((t, nh * HEAD_DIM), BF16),
        scratch_shapes=[pltpu.VMEM((HEAD_DIM + BF16_SUBLANES, seq), BF16),
                        pltpu.VMEM((2, seq, blk), F32)],
        compiler_params=_params("parallel", "parallel"),
        name="moba",
    )(b31, proj, proj, proj, bvec)


def _fox_gate_kernel(fl_ref, bf_ref, o_ref):
    x = fl_ref[...] + bf_ref[...]
    y = jnp.minimum(x, 0.0) - jnp.log1p(jnp.exp(-jnp.abs(x)))
    n = y.shape[1]
    lane = lax.broadcasted_iota(jnp.int32, y.shape, 1)
    sh = 1
    while sh < n:
        y = y + jnp.where(lane >= sh, pltpu.roll(y, sh, 1), 0.0)
        sh *= 2
    o_ref[...] = y


def _fox_gate(fl_t, b_forget, batch, seq):
    nh = fl_t.shape[0]
    return pl.pallas_call(
        _fox_gate_kernel,
        grid=(batch,),
        in_specs=[pl.BlockSpec((nh, seq), lambda b: (0, b)), _resident((nh, 1))],
        out_specs=pl.BlockSpec((nh, seq), lambda b: (0, b)),
        out_shape=jax.ShapeDtypeStruct((nh, batch * seq), F32),
        compiler_params=_params("parallel"),
        name="fox_gate",
    )(fl_t, b_forget)


def _fox_kernel(q_ref, k_ref, v_ref, f_ref, o_ref, ka_ref, vt_ref, s_ref):
    blk = FOX_BLOCK
    nb = k_ref.shape[0] // blk
    d = HEAD_DIM
    frow = f_ref[pl.ds(pl.program_id(1), 1), :] * (-LOG2E)
    lane = lax.broadcasted_iota(jnp.int32, (blk, d), 1)
    for n in range(nb):
        rows = slice(n * blk, (n + 1) * blk)
        fcol = jnp.transpose(jnp.broadcast_to(frow[:, rows], (d, blk)))
        hi = fcol.astype(BF16).astype(F32)
        mid = (fcol - hi).astype(BF16).astype(F32)
        lo = fcol - hi - mid
        aug = jnp.where(lane == 0, hi, jnp.where(lane == 1, mid, jnp.where(lane == 2, lo, 0.0)))
        ka_ref[rows, 0:d] = k_ref[rows, :]
        ka_ref[rows, d:2 * d] = aug.astype(BF16)
    _fill_values(v_ref, vt_ref, blk)
    ones = jnp.where(lane < 3, 1.0, 0.0).astype(BF16)
    krow = lax.broadcasted_iota(jnp.int32, (blk, blk), 0)
    qcol = lax.broadcasted_iota(jnp.int32, (blk, blk), 1)
    causal = jnp.where(qcol >= krow, 0.0, NEG)

    def prep(i):
        q = q_ref[i * blk:(i + 1) * blk, :].astype(F32)
        return jnp.concatenate([(q * (d ** -0.5 * LOG2E)).astype(BF16), ones], axis=1), None

    def bias(i, j, aux):
        return causal if j == i else None

    _block_attention(nb, blk, ka_ref, vt_ref, s_ref, o_ref, prep, bias)


def _fox(qkv, f, batch, seq):
    t = qkv.shape[0]
    nh = FOX_HEADS
    blk = FOX_BLOCK

    def head_spec(group):
        return pl.BlockSpec((seq, HEAD_DIM), lambda b, h: (b, group * nh + h))

    return pl.pallas_call(
        _fox_kernel,
        grid=(batch, nh),
        in_specs=[head_spec(0), head_spec(1), head_spec(2),
                  pl.BlockSpec((nh, seq), lambda b, h: (0, b))],
        out_specs=pl.BlockSpec((seq, HEAD_DIM), lambda b, h: (b, h)),
        out_shape=jax.ShapeDtypeStruct((t, nh * HEAD_DIM), BF16),
        scratch_shapes=[pltpu.VMEM((seq, 2 * HEAD_DIM), BF16),
                        pltpu.VMEM((HEAD_DIM + BF16_SUBLANES, seq), BF16),
                        pltpu.VMEM((2, seq, blk), F32)],
        compiler_params=_params("parallel", "parallel"),
        name="fox",
    )(qkv, qkv, qkv, f)


def _rotary_tables(seq):
    inv_freq = jnp.power(ROPE_BASE, -jnp.arange(0, HEAD_DIM, 2, dtype=F32) / HEAD_DIM)
    ang = jnp.arange(seq, dtype=F32)[:, None] * inv_freq[None, :]
    cos, sin = jnp.cos(ang), jnp.sin(ang)
    return jnp.concatenate([cos, cos], axis=-1), jnp.concatenate([-sin, sin], axis=-1)


def _t5_bucket(rel):
    n = jnp.maximum(rel, 0)
    max_exact = N_REL_BUCKETS // 2
    nf = jnp.maximum(n, 1).astype(F32)
    large = max_exact + (jnp.log(nf / max_exact) / math.log(REL_MAX_DISTANCE / max_exact)
                         * (N_REL_BUCKETS - max_exact)).astype(jnp.int32)
    large = jnp.minimum(large, N_REL_BUCKETS - 1)
    return jnp.where(n < max_exact, n, large)


def _moba_bias_vectors(rel_bias):
    blk = MOBA_BLOCK
    bias_t = rel_bias.astype(F32).T
    hot = _t5_bucket(jnp.arange(2 * blk))[:, None] == jnp.arange(N_REL_BUCKETS)
    near = jnp.sum(jnp.where(hot[None], bias_t[:, None, :], 0.0), axis=-1)
    far = bias_t[:, _t5_bucket(jnp.array(2 * blk, jnp.int32))]
    return far, near[:, None, :]


def kernel(x, rel_bias, norm_mix_pre, norm_mix_post, norm_ffn_pre, norm_ffn_post,
           even_w_in, even_gn, even_w_out, odd_w_in, odd_b_forget, odd_w_out,
           ffn_w_in, ffn_w_out):
    batch, seq, d = x.shape
    depth = norm_mix_pre.shape[0]
    xt = x.reshape(batch * seq, d)

    cos2, sin2 = _rotary_tables(seq)
    log_g = jnp.log1p(-jnp.power(2.0, -5.0 - jnp.arange(RET_HEADS, dtype=F32)))
    b31, bvec = _moba_bias_vectors(rel_bias)

    def row(v):
        return v.reshape(1, -1).astype(F32)

    for layer in range(depth):
        i = layer // 2
        if layer % 2 == 0:
            proj = _norm_proj(xt, row(norm_mix_pre[layer]), even_w_in[i].astype(BF16))
            ret = _retention(proj, log_g, cos2, sin2, row(even_gn[i]), batch, seq)
            mob = _moba(proj, b31, bvec, batch, seq, 4 * RET_HEADS)
            acts, w_mix = [ret, mob], even_w_out[i]
        else:
            w_in = jnp.pad(odd_w_in[i], ((0, 0), (0, HEAD_DIM - FOX_HEADS))).astype(BF16)
            qkv, fl_t = _norm_proj_gate(xt, row(norm_mix_pre[layer]), w_in, FOX_HEADS)
            f = _fox_gate(fl_t, odd_b_forget[i].reshape(-1, 1).astype(F32), batch, seq)
            acts, w_mix = [_fox(qkv, f, batch, seq)], odd_w_out[i]
        xt = _mix_ffn(acts, w_mix.astype(BF16), row(norm_mix_post[layer]), xt,
                      row(norm_ffn_pre[layer]), ffn_w_in[layer].astype(BF16),
                      ffn_w_out[layer].astype(BF16), row(norm_ffn_post[layer]))
    return xt.reshape(batch, seq, d)
```

```python
import functools
import math

import jax
import jax.numpy as jnp
from jax import lax
from jax.experimental import pallas as pl
from jax.experimental.pallas import tpu as pltpu

F32 = jnp.float32
BF16 = jnp.bfloat16

HEAD_DIM = 128
RET_HEADS = 4
MOBA_HEADS = 4
FOX_HEADS = 8
RET_CHUNK = 128
MOBA_BLOCK = 256
MOBA_TOPK = 3
N_REL_BUCKETS = 32
REL_MAX_DISTANCE = 128
ROPE_BASE = 10000.0
RMS_EPS = 1e-6
GN_EPS = 1e-5
NEG = -1e30
LOG2E = math.log2(math.e)

V7X_VMEM_BYTES = 64 * 1024 * 1024
VMEM_LIMIT = V7X_VMEM_BYTES - 8 * 1024 * 1024

ROW_TILE = 512
MIX_ROWS = 512
MIX_SUB_ROWS = 512
COL_CHUNK = 512
FFN_CHUNK = 256
FOX_BLOCK = 256
BF16_SUBLANES = 16
MOBA_ISSUE = (2, 3)
FOX_ISSUE = (4, 2)


def _params(*sem, flags=None):
    return pltpu.CompilerParams(dimension_semantics=sem, vmem_limit_bytes=VMEM_LIMIT,
                                flags=flags)


def _resident(shape):
    nd = len(shape)
    return pl.BlockSpec(shape, lambda *_: (0,) * nd, pipeline_mode=pl.Buffered(1))


def _rms(x, g):
    return x * lax.rsqrt(jnp.mean(x * x, axis=-1, keepdims=True) + RMS_EPS) * g


def _dot(a, b):
    return jnp.dot(a, b, preferred_element_type=F32)


def _dot_nt(a, b):
    return lax.dot_general(a, b, (((1,), (1,)), ((), ())), preferred_element_type=F32)


def _norm_proj_kernel(x_ref, g_ref, w_ref, o_ref):
    hn = _rms(x_ref[...], g_ref[...]).astype(BF16)
    for c in range(o_ref.shape[1] // COL_CHUNK):
        sl = slice(c * COL_CHUNK, (c + 1) * COL_CHUNK)
        o_ref[:, sl] = _dot(hn, w_ref[:, sl]).astype(o_ref.dtype)


def _norm_proj(x, g, w):
    t, d = x.shape
    n = w.shape[1]
    return pl.pallas_call(
        _norm_proj_kernel,
        grid=(t // ROW_TILE,),
        in_specs=[pl.BlockSpec((ROW_TILE, d), lambda i: (i, 0)),
                  _resident((1, d)), _resident((d, n))],
        out_specs=pl.BlockSpec((ROW_TILE, n), lambda i: (i, 0)),
        out_shape=jax.ShapeDtypeStruct((t, n), BF16),
        compiler_params=_params("parallel"),
        name="norm_proj",
    )(x, g, w)


def _norm_proj_gate_kernel(x_ref, g_ref, w_ref, o_ref, f_ref):
    hn = _rms(x_ref[...], g_ref[...]).astype(BF16)
    n = o_ref.shape[1]
    for c in range(n // COL_CHUNK):
        sl = slice(c * COL_CHUNK, (c + 1) * COL_CHUNK)
        o_ref[:, sl] = _dot(hn, w_ref[:, sl]).astype(o_ref.dtype)
    fl = _dot(hn, w_ref[:, n:])
    f_ref[...] = jnp.transpose(fl)[0:f_ref.shape[0], :]


def _norm_proj_gate(x, g, w, nh):
    t, d = x.shape
    n = w.shape[1] - HEAD_DIM
    return pl.pallas_call(
        _norm_proj_gate_kernel,
        grid=(t // ROW_TILE,),
        in_specs=[pl.BlockSpec((ROW_TILE, d), lambda i: (i, 0)),
                  _resident((1, d)), _resident(w.shape)],
        out_specs=[pl.BlockSpec((ROW_TILE, n), lambda i: (i, 0)),
                   pl.BlockSpec((nh, ROW_TILE), lambda i: (0, i))],
        out_shape=[jax.ShapeDtypeStruct((t, n), BF16),
                   jax.ShapeDtypeStruct((nh, t), F32)],
        compiler_params=_params("parallel"),
        name="norm_proj_gate",
    )(x, g, w)


def _mix_ffn_kernel(n_in, *refs):
    a_refs = refs[:n_in]
    wmix_ref, gmix_ref, x_ref, gpre_ref, wffn_ref, wo_ref, gpost_ref, o_ref = refs[n_in:]
    hidden = wo_ref.shape[0]
    subs = [slice(r, r + MIX_SUB_ROWS) for r in range(0, x_ref.shape[0], MIX_SUB_ROWS)]

    xs, hns = [], []
    for rows in subs:
        m = None
        off = 0
        for a_ref in a_refs:
            width = a_ref.shape[1]
            part = _dot(a_ref[rows, :], wmix_ref[off:off + width, :])
            m = part if m is None else m + part
            off += width
        x = x_ref[rows, :] + _rms(m, gmix_ref[...])
        xs.append(x)
        hns.append(_rms(x, gpre_ref[...]).astype(BF16))
    for rows, x, hn in zip(subs, xs, hns):
        y = jnp.zeros(x.shape, F32)
        for c in range(hidden // FFN_CHUNK):
            lo = c * FFN_CHUNK
            gate = _dot(hn, wffn_ref[:, lo:lo + FFN_CHUNK])
            up = _dot(hn, wffn_ref[:, hidden + lo:hidden + lo + FFN_CHUNK])
            act = (gate * jax.nn.sigmoid(gate) * up).astype(BF16)
            y = y + _dot(act, wo_ref[lo:lo + FFN_CHUNK, :])
        o_ref[rows, :] = x + _rms(y, gpost_ref[...])


def _mix_ffn(acts, w_mix, g_mix, x, g_pre, w_ffn, w_o, g_post):
    t, d = x.shape
    in_specs = [pl.BlockSpec((MIX_ROWS, a.shape[1]), lambda i: (i, 0)) for a in acts]
    in_specs += [_resident(w_mix.shape), _resident((1, d)),
                 pl.BlockSpec((MIX_ROWS, d), lambda i: (i, 0)),
                 _resident((1, d)), _resident(w_ffn.shape), _resident(w_o.shape),
                 _resident((1, d))]
    return pl.pallas_call(
        functools.partial(_mix_ffn_kernel, len(acts)),
        grid=(t // MIX_ROWS,),
        in_specs=in_specs,
        out_specs=pl.BlockSpec((MIX_ROWS, d), lambda i: (i, 0)),
        out_shape=jax.ShapeDtypeStruct((t, d), F32),
        compiler_params=_params("parallel"),
        name="mix_ffn",
    )(*acts, w_mix, g_mix, x, g_pre, w_ffn, w_o, g_post)


def _retention_kernel(lg_ref, q_ref, k_ref, v_ref, g_ref, cos_ref, sin_ref, gn_ref, o_ref):
    c = RET_CHUNK
    seq = q_ref.shape[0]
    lg = lg_ref[pl.program_id(1)]
    row = lax.broadcasted_iota(jnp.int32, (c, c), 0).astype(F32)
    col = lax.broadcasted_iota(jnp.int32, (c, c), 1).astype(F32)
    diff = row - col
    inner = jnp.where(diff >= 0, jnp.exp(lg * jnp.maximum(diff, 0.0)), 0.0)
    idx = lax.broadcasted_iota(jnp.int32, (c, HEAD_DIM), 0).astype(F32)
    q_dec = jnp.exp(lg * (idx + 1.0))
    k_dec = jnp.exp(lg * (c - 1.0 - idx))
    chunk_dec = jnp.exp(lg * jnp.full((1, HEAD_DIM), float(c), F32))
    gn_w = gn_ref[...]
    scale = HEAD_DIM ** -0.5

    def rot(t, cos, sin):
        return t * cos + pltpu.roll(t, HEAD_DIM // 2, 1) * sin

    def step(n, state):
        sl = slice(n * c, (n + 1) * c)
        cos = cos_ref[sl, :]
        sin = sin_ref[sl, :]
        q = rot(q_ref[sl, :].astype(F32), cos, sin) * scale
        k = rot(k_ref[sl, :].astype(F32), cos, sin)
        v = v_ref[sl, :]
        qb = q.astype(BF16)
        sc = _dot_nt(qb, k.astype(BF16)) * inner
        o = _dot(sc.astype(BF16), v) + _dot(qb, state.astype(BF16)) * q_dec
        kd_t = jnp.transpose(k * k_dec).astype(BF16)
        state = state * chunk_dec + _dot(kd_t, v)
        mu = jnp.mean(o, axis=-1, keepdims=True)
        var = jnp.mean(jnp.square(o - mu), axis=-1, keepdims=True)
        y = (o - mu) * lax.rsqrt(var + GN_EPS) * gn_w
        gate = g_ref[sl, :].astype(F32)
        o_ref[sl, :] = (y * (gate * jax.nn.sigmoid(gate))).astype(o_ref.dtype)
        return state

    state = jnp.zeros((HEAD_DIM, HEAD_DIM), F32)
    for n in range(seq // c):
        state = step(n, state)


def _retention(proj, log_g, cos2, sin2, gn_w, batch, seq):
    t = proj.shape[0]
    nh = RET_HEADS

    def head_spec(group):
        return pl.BlockSpec((seq, HEAD_DIM), lambda b, h: (b, group * nh + h))

    return pl.pallas_call(
        _retention_kernel,
        grid=(batch, nh),
        in_specs=[pl.BlockSpec(memory_space=pltpu.SMEM),
                  head_spec(0), head_spec(1), head_spec(2), head_spec(3),
                  _resident((seq, HEAD_DIM)), _resident((seq, HEAD_DIM)),
                  pl.BlockSpec((1, HEAD_DIM), lambda b, h: (0, h))],
        out_specs=pl.BlockSpec((seq, HEAD_DIM), lambda b, h: (b, h)),
        out_shape=jax.ShapeDtypeStruct((t, nh * HEAD_DIM), BF16),
        compiler_params=_params("parallel", "parallel"),
        name="retention",
    )(log_g, proj, proj, proj, proj, cos2, sin2, gn_w)


def _group_reduce(x, op):
    r, c = x.shape
    g = r // 8
    if g % 4 == 0:
        return op(op(x.reshape(4, g // 4, 8, c), axis=1), axis=0)
    return op(x.reshape(g, 8, c), axis=0)


def _block_attention(nb, blk, issue, keys_ref, vt_ref, s_ref, o_ref, prep, bias):
    d = HEAD_DIM
    chunk, lag = issue

    def chunks(i):
        return [(lo, min(lo + chunk, i + 1)) for lo in range(0, i + 1, chunk)]

    def score_chunk(i, lo, hi, qa, aux, m8):
        s = _dot_nt(keys_ref[lo * blk:hi * blk, :], qa)
        for j in range(lo, hi):
            t = s[(j - lo) * blk:(j - lo + 1) * blk, :]
            b = bias(i, j, aux)
            if b is not None:
                t = t + b
            s_ref[i % 2, j * blk:(j + 1) * blk, :] = t
            t8 = _group_reduce(t, jnp.max)
            m8 = t8 if m8 is None else jnp.maximum(m8, t8)
        return m8

    def attend_chunk(i, lo, hi, m, acc):
        ps = [jnp.exp2(s_ref[i % 2, j * blk:(j + 1) * blk, :] - m).astype(BF16)
              for j in range(lo, hi)]
        p = ps[0] if len(ps) == 1 else jnp.concatenate(ps, axis=0)
        part = _dot(vt_ref[:, lo * blk:hi * blk], p)
        return part if acc is None else acc + part

    qa, aux = prep(nb - 1)
    m8 = None
    for lo, hi in chunks(nb - 1):
        m8 = score_chunk(nb - 1, lo, hi, qa, aux, m8)
    for i in reversed(range(nb)):
        m = jnp.max(m8, axis=0, keepdims=True)
        if i > 0:
            qa, aux = prep(i - 1)
        ahead = chunks(i - 1) if i > 0 else []
        here = chunks(i)
        m8 = None
        acc = None
        snaps = []
        for c in range(len(here)):
            snap = snaps[-lag] if len(snaps) >= lag else None
            m_c = m if snap is None else jnp.minimum(m, jnp.maximum(m, snap[0:1, :]))
            if c < len(ahead):
                m8 = score_chunk(i - 1, *ahead[c], qa, aux, m8)
                snaps.append(m8)
            acc = attend_chunk(i, *here[c], m_c, acc)
        out = acc[0:d, :] / acc[d:d + 1, :]
        o_ref[i * blk:(i + 1) * blk, :] = jnp.transpose(out).astype(o_ref.dtype)


def _fill_values(v_ref, vt_ref, blk):
    d = HEAD_DIM
    seq = v_ref.shape[0]
    for n in range(seq // blk):
        rows = slice(n * blk, (n + 1) * blk)
        vt_ref[0:d, rows] = jnp.transpose(v_ref[rows, :].astype(F32)).astype(BF16)
    first = lax.broadcasted_iota(jnp.int32, (vt_ref.shape[0] - d, seq), 0) == 0
    vt_ref[d:, :] = jnp.where(first, 1.0, 0.0).astype(BF16)


def _moba_kernel(b31_ref, q_ref, k_ref, v_ref, bvec_ref, o_ref, vt_ref, s_ref):
    blk = MOBA_BLOCK
    nb = k_ref.shape[0] // blk
    b31 = b31_ref[pl.program_id(1)]

    _fill_values(v_ref, vt_ref, blk)
    means = [jnp.mean(k_ref[n * blk:(n + 1) * blk, :].astype(F32), axis=0, keepdims=True)
             for n in range(nb)]
    km = jnp.concatenate(means, axis=0)
    km_hi = km.astype(BF16)
    km_lo = (km - km_hi.astype(F32)).astype(BF16)
    row = lax.broadcasted_iota(jnp.int32, (nb, blk), 0).astype(F32)
    bvec = jnp.broadcast_to(bvec_ref[...] * LOG2E, (blk, 2 * blk))
    toep = pltpu.roll(bvec, 0, 1, stride=1, stride_axis=0)
    krow = lax.broadcasted_iota(jnp.int32, (blk, blk), 0)
    qcol = lax.broadcasted_iota(jnp.int32, (blk, blk), 1)
    town = jnp.where(qcol >= krow, toep[:, 0:blk], NEG)
    tadj = toep[:, blk:2 * blk]

    def prep(i):
        q = q_ref[i * blk:(i + 1) * blk, :]
        gate = _dot_nt(km_hi, q) + _dot_nt(km_lo, q)
        valid = row < float(i)
        gm = jnp.where(valid, gate, -jnp.inf)
        sel = jnp.full(gate.shape, NEG, F32)
        for _ in range(min(MOBA_TOPK, nb)):
            mx = jnp.max(gm, axis=0, keepdims=True)
            first = jnp.min(jnp.where(gm == mx, row, float(nb)), axis=0, keepdims=True)
            pick = row == first
            sel = jnp.where(jnp.logical_and(pick, valid), 0.0, sel)
            gm = jnp.where(pick, -jnp.inf, gm)

        qs = (q.astype(F32) * (HEAD_DIM ** -0.5 * LOG2E)).astype(BF16)
        return qs, (sel, sel + b31 * LOG2E)

    def bias(i, j, aux):
        sel, far = aux
        if j == i:
            return town
        if j == i - 1:
            return tadj + sel[j:j + 1, :]
        return far[j:j + 1, :]

    _block_attention(nb, blk, MOBA_ISSUE, k_ref, vt_ref, s_ref, o_ref, prep, bias)


def _moba(proj, b31, bvec, batch, seq, col0):
    t = proj.shape[0]
    nh = MOBA_HEADS
    blk = MOBA_BLOCK

    def head_spec(group):
        return pl.BlockSpec((seq, HEAD_DIM), lambda b, h: (b, col0 + group * nh + h))

    return pl.pallas_call(
        _moba_kernel,
        grid=(batch, nh),
        in_specs=[pl.BlockSpec(memory_space=pltpu.SMEM),
                  head_spec(0), head_spec(1), head_spec(2),
                  pl.BlockSpec((None, 1, 2 * blk), lambda b, h: (h, 0, 0))],
        out_specs=pl.BlockSpec((seq, HEAD_DIM), lambda b, h: (b, h)),
        out_shape=jax.ShapeDtypeStruct((t, nh * HEAD_DIM), BF16),
        scratch_shapes=[pltpu.VMEM((HEAD_DIM + BF16_SUBLANES, seq), BF16),
                        pltpu.VMEM((2, seq, blk), F32)],
        compiler_params=_params("parallel", "parallel"),
        name="moba",
    )(b31, proj, proj, proj, bvec)


def _fox_gate_kernel(fl_ref, bf_ref, o_ref):
    x = fl_ref[...] + bf_ref[...]
    y = jnp.minimum(x, 0.0) - jnp.log1p(jnp.exp(-jnp.abs(x)))
    n = y.shape[1]
    lane = lax.broadcasted_iota(jnp.int32, y.shape, 1)
    sh = 1
    while sh < n:
        y = y + jnp.where(lane >= sh, pltpu.roll(y, sh, 1), 0.0)
        sh *= 2
    o_ref[...] = y


def _fox_gate(fl_t, b_forget, batch, seq):
    nh = fl_t.shape[0]
    return pl.pallas_call(
        _fox_gate_kernel,
        grid=(batch,),
        in_specs=[pl.BlockSpec((nh, seq), lambda b: (0, b)), _resident((nh, 1))],
        out_specs=pl.BlockSpec((nh, seq), lambda b: (0, b)),
        out_shape=jax.ShapeDtypeStruct((nh, batch * seq), F32),
        compiler_params=_params("parallel"),
        name="fox_gate",
    )(fl_t, b_forget)


def _fox_kernel(q_ref, k_ref, v_ref, f_ref, o_ref, ka_ref, vt_ref, s_ref):
    blk = FOX_BLOCK
    nb = k_ref.shape[0] // blk
    d = HEAD_DIM
    frow = f_ref[pl.ds(pl.program_id(1), 1), :] * (-LOG2E)
    hi = frow.astype(BF16).astype(F32)
    mid = (frow - hi).astype(BF16).astype(F32)
    lo = frow - hi - mid
    sub = lax.broadcasted_iota(jnp.int32, (8, frow.shape[1]), 0)
    top = jnp.where(sub == 0, hi, jnp.where(sub == 1, mid, jnp.where(sub == 2, lo, 0.0)))
    pad = jnp.zeros((d - 8, blk), F32)
    lane = lax.broadcasted_iota(jnp.int32, (blk, d), 1)
    for n in range(nb):
        rows = slice(n * blk, (n + 1) * blk)
        aug = jnp.transpose(jnp.concatenate([top[:, rows], pad], axis=0))
        ka_ref[rows, 0:d] = k_ref[rows, :]
        ka_ref[rows, d:2 * d] = aug.astype(BF16)
    _fill_values(v_ref, vt_ref, blk)
    ones = jnp.where(lane < 3, 1.0, 0.0).astype(BF16)
    krow = lax.broadcasted_iota(jnp.int32, (blk, blk), 0)
    qcol = lax.broadcasted_iota(jnp.int32, (blk, blk), 1)
    causal = jnp.where(qcol >= krow, 0.0, NEG)

    def prep(i):
        q = q_ref[i * blk:(i + 1) * blk, :].astype(F32)
        return jnp.concatenate([(q * (d ** -0.5 * LOG2E)).astype(BF16), ones], axis=1), None

    def bias(i, j, aux):
        return causal if j == i else None

    _block_attention(nb, blk, FOX_ISSUE, ka_ref, vt_ref, s_ref, o_ref, prep, bias)


def _fox(qkv, f, batch, seq):
    t = qkv.shape[0]
    nh = FOX_HEADS
    blk = FOX_BLOCK

    def head_spec(group):
        return pl.BlockSpec((seq, HEAD_DIM), lambda b, h: (b, group * nh + h))

    return pl.pallas_call(
        _fox_kernel,
        grid=(batch, nh),
        in_specs=[head_spec(0), head_spec(1), head_spec(2),
                  pl.BlockSpec((nh, seq), lambda b, h: (0, b))],
        out_specs=pl.BlockSpec((seq, HEAD_DIM), lambda b, h: (b, h)),
        out_shape=jax.ShapeDtypeStruct((t, nh * HEAD_DIM), BF16),
        scratch_shapes=[pltpu.VMEM((seq, 2 * HEAD_DIM), BF16),
                        pltpu.VMEM((HEAD_DIM + BF16_SUBLANES, seq), BF16),
                        pltpu.VMEM((2, seq, blk), F32)],
        compiler_params=_params("parallel", "parallel"),
        name="fox",
    )(qkv, qkv, qkv, f)


def _rotary_tables(seq):
    inv_freq = jnp.power(ROPE_BASE, -jnp.arange(0, HEAD_DIM, 2, dtype=F32) / HEAD_DIM)
    ang = jnp.arange(seq, dtype=F32)[:, None] * inv_freq[None, :]
    cos, sin = jnp.cos(ang), jnp.sin(ang)
    return jnp.concatenate([cos, cos], axis=-1), jnp.concatenate([-sin, sin], axis=-1)


def _t5_bucket(rel):
    n = jnp.maximum(rel, 0)
    max_exact = N_REL_BUCKETS // 2
    nf = jnp.maximum(n, 1).astype(F32)
    large = max_exact + (jnp.log(nf / max_exact) / math.log(REL_MAX_DISTANCE / max_exact)
                         * (N_REL_BUCKETS - max_exact)).astype(jnp.int32)
    large = jnp.minimum(large, N_REL_BUCKETS - 1)
    return jnp.where(n < max_exact, n, large)


def _moba_bias_vectors(rel_bias):
    blk = MOBA_BLOCK
    bias_t = rel_bias.astype(F32).T
    hot = _t5_bucket(jnp.arange(2 * blk))[:, None] == jnp.arange(N_REL_BUCKETS)
    near = jnp.sum(jnp.where(hot[None], bias_t[:, None, :], 0.0), axis=-1)
    far = bias_t[:, _t5_bucket(jnp.array(2 * blk, jnp.int32))]
    return far, near[:, None, :]


def kernel(x, rel_bias, norm_mix_pre, norm_mix_post, norm_ffn_pre, norm_ffn_post,
           even_w_in, even_gn, even_w_out, odd_w_in, odd_b_forget, odd_w_out,
           ffn_w_in, ffn_w_out):
    batch, seq, d = x.shape
    depth = norm_mix_pre.shape[0]
    xt = x.reshape(batch * seq, d)

    cos2, sin2 = _rotary_tables(seq)
    log_g = jnp.log1p(-jnp.power(2.0, -5.0 - jnp.arange(RET_HEADS, dtype=F32)))
    b31, bvec = _moba_bias_vectors(rel_bias)

    def row(v):
        return v.reshape(1, -1).astype(F32)

    for layer in range(depth):
        i = layer // 2
        if layer % 2 == 0:
            proj = _norm_proj(xt, row(norm_mix_pre[layer]), even_w_in[i].astype(BF16))
            ret = _retention(proj, log_g, cos2, sin2, row(even_gn[i]), batch, seq)
            mob = _moba(proj, b31, bvec, batch, seq, 4 * RET_HEADS)
            acts, w_mix = [ret, mob], even_w_out[i]
        else:
            w_in = jnp.pad(odd_w_in[i], ((0, 0), (0, HEAD_DIM - FOX_HEADS))).astype(BF16)
            qkv, fl_t = _norm_proj_gate(xt, row(norm_mix_pre[layer]), w_in, FOX_HEADS)
            f = _fox_gate(fl_t, odd_b_forget[i].reshape(-1, 1).astype(F32), batch, seq)
            acts, w_mix = [_fox(qkv, f, batch, seq)], odd_w_out[i]
        xt = _mix_ffn(acts, w_mix.astype(BF16), row(norm_mix_post[layer]), xt,
                      row(norm_ffn_pre[layer]), ffn_w_in[layer].astype(BF16),
                      ffn_w_out[layer].astype(BF16), row(norm_ffn_post[layer]))
    return xt.reshape(batch, seq, d)
```

```python
import functools
import math

import jax
import jax.numpy as jnp
from jax import lax
from jax.experimental import pallas as pl
from jax.experimental.pallas import tpu as pltpu

F32 = jnp.float32
BF16 = jnp.bfloat16

HEAD_DIM = 128
RET_HEADS = 4
MOBA_HEADS = 4
FOX_HEADS = 8
RET_CHUNK = 128
MOBA_BLOCK = 256
MOBA_TOPK = 3
N_REL_BUCKETS = 32
REL_MAX_DISTANCE = 128
ROPE_BASE = 10000.0
RMS_EPS = 1e-6
GN_EPS = 1e-5
NEG = -1e30
LOG2E = math.log2(math.e)

V7X_VMEM_BYTES = 64 * 1024 * 1024
VMEM_LIMIT = V7X_VMEM_BYTES - 8 * 1024 * 1024

ROW_TILE = 1024
MIX_ROWS = 1024
SUB_ROWS = 512
COL_CHUNK = 512
FFN_CHUNK = 256
FOX_BLOCK = 256
BF16_SUBLANES = 16
MOBA_ISSUE = (2, 3)
FOX_ISSUE = (4, 2)


def _params(*sem, flags=None):
    return pltpu.CompilerParams(dimension_semantics=sem, vmem_limit_bytes=VMEM_LIMIT,
                                flags=flags)


def _resident(shape):
    nd = len(shape)
    return pl.BlockSpec(shape, lambda *_: (0,) * nd, pipeline_mode=pl.Buffered(1))


def _rms(x, g):
    return x * lax.rsqrt(jnp.mean(x * x, axis=-1, keepdims=True) + RMS_EPS) * g


def _dot(a, b):
    return jnp.dot(a, b, preferred_element_type=F32)


def _dot_nt(a, b):
    return lax.dot_general(a, b, (((1,), (1,)), ((), ())), preferred_element_type=F32)


def _sub_tiles(rows, sub):
    return [slice(r, r + sub) for r in range(0, rows, sub)]


def _norm_proj_kernel(x_ref, g_ref, w_ref, o_ref):
    subs = _sub_tiles(x_ref.shape[0], SUB_ROWS)
    hns = [_rms(x_ref[rows, :], g_ref[...]).astype(BF16) for rows in subs]
    for rows, hn in zip(subs, hns):
        for c in range(o_ref.shape[1] // COL_CHUNK):
            sl = slice(c * COL_CHUNK, (c + 1) * COL_CHUNK)
            o_ref[rows, sl] = _dot(hn, w_ref[:, sl]).astype(o_ref.dtype)


def _norm_proj(x, g, w):
    t, d = x.shape
    n = w.shape[1]
    return pl.pallas_call(
        _norm_proj_kernel,
        grid=(t // ROW_TILE,),
        in_specs=[pl.BlockSpec((ROW_TILE, d), lambda i: (i, 0)),
                  _resident((1, d)), _resident((d, n))],
        out_specs=pl.BlockSpec((ROW_TILE, n), lambda i: (i, 0)),
        out_shape=jax.ShapeDtypeStruct((t, n), BF16),
        compiler_params=_params("parallel"),
        name="norm_proj",
    )(x, g, w)


def _norm_proj_gate_kernel(x_ref, g_ref, w_ref, o_ref, f_ref):
    n = o_ref.shape[1]
    subs = _sub_tiles(x_ref.shape[0], SUB_ROWS)
    hns = [_rms(x_ref[rows, :], g_ref[...]).astype(BF16) for rows in subs]
    for rows, hn in zip(subs, hns):
        for c in range(n // COL_CHUNK):
            sl = slice(c * COL_CHUNK, (c + 1) * COL_CHUNK)
            o_ref[rows, sl] = _dot(hn, w_ref[:, sl]).astype(o_ref.dtype)
        fl = _dot(hn, w_ref[:, n:])
        f_ref[:, rows] = jnp.transpose(fl)[0:f_ref.shape[0], :]


def _norm_proj_gate(x, g, w, nh):
    t, d = x.shape
    n = w.shape[1] - HEAD_DIM
    return pl.pallas_call(
        _norm_proj_gate_kernel,
        grid=(t // ROW_TILE,),
        in_specs=[pl.BlockSpec((ROW_TILE, d), lambda i: (i, 0)),
                  _resident((1, d)), _resident(w.shape)],
        out_specs=[pl.BlockSpec((ROW_TILE, n), lambda i: (i, 0)),
                   pl.BlockSpec((nh, ROW_TILE), lambda i: (0, i))],
        out_shape=[jax.ShapeDtypeStruct((t, n), BF16),
                   jax.ShapeDtypeStruct((nh, t), F32)],
        compiler_params=_params("parallel"),
        name="norm_proj_gate",
    )(x, g, w)


def _mix_ffn_kernel(n_in, *refs):
    a_refs = refs[:n_in]
    wmix_ref, gmix_ref, x_ref, gpre_ref, wffn_ref, wo_ref, gpost_ref, o_ref = refs[n_in:]
    hidden = wo_ref.shape[0]
    subs = _sub_tiles(x_ref.shape[0], SUB_ROWS)

    xs, hns = [], []
    for rows in subs:
        m = None
        off = 0
        for a_ref in a_refs:
            width = a_ref.shape[1]
            part = _dot(a_ref[rows, :], wmix_ref[off:off + width, :])
            m = part if m is None else m + part
            off += width
        x = x_ref[rows, :] + _rms(m, gmix_ref[...])
        xs.append(x)
        hns.append(_rms(x, gpre_ref[...]).astype(BF16))
    for rows, x, hn in zip(subs, xs, hns):
        y = jnp.zeros(x.shape, F32)
        for c in range(hidden // FFN_CHUNK):
            lo = c * FFN_CHUNK
            gate = _dot(hn, wffn_ref[:, lo:lo + FFN_CHUNK])
            up = _dot(hn, wffn_ref[:, hidden + lo:hidden + lo + FFN_CHUNK])
            act = (gate * jax.nn.sigmoid(gate) * up).astype(BF16)
            y = y + _dot(act, wo_ref[lo:lo + FFN_CHUNK, :])
        o_ref[rows, :] = x + _rms(y, gpost_ref[...])


def _mix_ffn(acts, w_mix, g_mix, x, g_pre, w_ffn, w_o, g_post):
    t, d = x.shape
    in_specs = [pl.BlockSpec((MIX_ROWS, a.shape[1]), lambda i: (i, 0)) for a in acts]
    in_specs += [_resident(w_mix.shape), _resident((1, d)),
                 pl.BlockSpec((MIX_ROWS, d), lambda i: (i, 0)),
                 _resident((1, d)), _resident(w_ffn.shape), _resident(w_o.shape),
                 _resident((1, d))]
    return pl.pallas_call(
        functools.partial(_mix_ffn_kernel, len(acts)),
        grid=(t // MIX_ROWS,),
        in_specs=in_specs,
        out_specs=pl.BlockSpec((MIX_ROWS, d), lambda i: (i, 0)),
        out_shape=jax.ShapeDtypeStruct((t, d), F32),
        compiler_params=_params("parallel"),
        name="mix_ffn",
    )(*acts, w_mix, g_mix, x, g_pre, w_ffn, w_o, g_post)


def _retention_kernel(lg_ref, q_ref, k_ref, v_ref, g_ref, cos_ref, sin_ref, gn_ref, o_ref):
    c = RET_CHUNK
    seq = q_ref.shape[0]
    lg = lg_ref[pl.program_id(1)]
    row = lax.broadcasted_iota(jnp.int32, (c, c), 0).astype(F32)
    col = lax.broadcasted_iota(jnp.int32, (c, c), 1).astype(F32)
    diff = row - col
    inner = jnp.where(diff >= 0, jnp.exp(lg * jnp.maximum(diff, 0.0)), 0.0)
    idx = lax.broadcasted_iota(jnp.int32, (c, HEAD_DIM), 0).astype(F32)
    q_dec = jnp.exp(lg * (idx + 1.0))
    k_dec = jnp.exp(lg * (c - 1.0 - idx))
    chunk_dec = jnp.exp(lg * jnp.full((1, HEAD_DIM), float(c), F32))
    gn_w = gn_ref[...]
    scale = HEAD_DIM ** -0.5

    def rot(t, cos, sin):
        return t * cos + pltpu.roll(t, HEAD_DIM // 2, 1) * sin

    def head(n):
        sl = slice(n * c, (n + 1) * c)
        cos = cos_ref[sl, :]
        sin = sin_ref[sl, :]
        q = rot(q_ref[sl, :].astype(F32), cos, sin) * scale
        k = rot(k_ref[sl, :].astype(F32), cos, sin)
        qb = q.astype(BF16)
        return qb, k, _dot_nt(qb, k.astype(BF16))

    def tail(n, state, qb, k, sc):
        sl = slice(n * c, (n + 1) * c)
        v = v_ref[sl, :]
        kd_t = jnp.transpose(k * k_dec).astype(BF16)
        kv = _dot(kd_t, v)
        cross = _dot(qb, state.astype(BF16))
        o = _dot((sc * inner).astype(BF16), v) + cross * q_dec
        mu = jnp.mean(o, axis=-1, keepdims=True)
        var = jnp.mean(jnp.square(o - mu), axis=-1, keepdims=True)
        y = (o - mu) * lax.rsqrt(var + GN_EPS) * gn_w
        gate = g_ref[sl, :].astype(F32)
        o_ref[sl, :] = (y * (gate * jax.nn.sigmoid(gate))).astype(o_ref.dtype)
        return state * chunk_dec + kv

    state = jnp.zeros((HEAD_DIM, HEAD_DIM), F32)
    n_chunks = seq // c
    ahead = head(0)
    for n in range(n_chunks):
        cur = ahead
        if n + 1 < n_chunks:
            ahead = head(n + 1)
        state = tail(n, state, *cur)


def _retention(proj, log_g, cos2, sin2, gn_w, batch, seq):
    t = proj.shape[0]
    nh = RET_HEADS

    def head_spec(group):
        return pl.BlockSpec((seq, HEAD_DIM), lambda b, h: (b, group * nh + h))

    return pl.pallas_call(
        _retention_kernel,
        grid=(batch, nh),
        in_specs=[pl.BlockSpec(memory_space=pltpu.SMEM),
                  head_spec(0), head_spec(1), head_spec(2), head_spec(3),
                  _resident((seq, HEAD_DIM)), _resident((seq, HEAD_DIM)),
                  pl.BlockSpec((1, HEAD_DIM), lambda b, h: (0, h))],
        out_specs=pl.BlockSpec((seq, HEAD_DIM), lambda b, h: (b, h)),
        out_shape=jax.ShapeDtypeStruct((t, nh * HEAD_DIM), BF16),
        compiler_params=_params("parallel", "parallel"),
        name="retention",
    )(log_g, proj, proj, proj, proj, cos2, sin2, gn_w)


def _group_reduce(x, op):
    r, c = x.shape
    g = r // 8
    if g % 4 == 0:
        return op(op(x.reshape(4, g // 4, 8, c), axis=1), axis=0)
    return op(x.reshape(g, 8, c), axis=0)


def _block_attention(nb, blk, issue, keys_ref, vt_ref, s_ref, o_ref, prep, bias):
    d = HEAD_DIM
    chunk, lag = issue

    def chunks(i):
        return [(lo, min(lo + chunk, i + 1)) for lo in range(0, i + 1, chunk)]

    def score_chunk(i, lo, hi, qa, aux, m8):
        s = _dot_nt(keys_ref[lo * blk:hi * blk, :], qa)
        for j in range(lo, hi):
            t = s[(j - lo) * blk:(j - lo + 1) * blk, :]
            b = bias(i, j, aux)
            if b is not None:
                t = t + b
            s_ref[i % 2, j * blk:(j + 1) * blk, :] = t
            t8 = _group_reduce(t, jnp.max)
            m8 = t8 if m8 is None else jnp.maximum(m8, t8)
        return m8

    def attend_chunk(i, lo, hi, m, acc):
        ps = [jnp.exp2(s_ref[i % 2, j * blk:(j + 1) * blk, :] - m).astype(BF16)
              for j in range(lo, hi)]
        p = ps[0] if len(ps) == 1 else jnp.concatenate(ps, axis=0)
        part = _dot(vt_ref[:, lo * blk:hi * blk], p)
        return part if acc is None else acc + part

    qa, aux = prep(nb - 1)
    m8 = None
    for lo, hi in chunks(nb - 1):
        m8 = score_chunk(nb - 1, lo, hi, qa, aux, m8)
    for i in reversed(range(nb)):
        m = jnp.max(m8, axis=0, keepdims=True)
        if i > 0:
            qa, aux = prep(i - 1)
        ahead = chunks(i - 1) if i > 0 else []
        here = chunks(i)
        m8 = None
        acc = None
        snaps = []
        for c in range(len(here)):
            snap = snaps[-lag] if len(snaps) >= lag else None
            m_c = m if snap is None else jnp.minimum(m, jnp.maximum(m, snap[0:1, :]))
            if c < len(ahead):
                m8 = score_chunk(i - 1, *ahead[c], qa, aux, m8)
                snaps.append(m8)
            acc = attend_chunk(i, *here[c], m_c, acc)
        out = acc[0:d, :] / acc[d:d + 1, :]
        o_ref[i * blk:(i + 1) * blk, :] = jnp.transpose(out).astype(o_ref.dtype)


def _fill_values(v_ref, vt_ref, blk):
    d = HEAD_DIM
    seq = v_ref.shape[0]
    for n in range(seq // blk):
        rows = slice(n * blk, (n + 1) * blk)
        vt_ref[0:d, rows] = jnp.transpose(v_ref[rows, :].astype(F32)).astype(BF16)
    first = lax.broadcasted_iota(jnp.int32, (vt_ref.shape[0] - d, seq), 0) == 0
    vt_ref[d:, :] = jnp.where(first, 1.0, 0.0).astype(BF16)


def _moba_kernel(b31_ref, q_ref, k_ref, v_ref, bvec_ref, o_ref, vt_ref, s_ref):
    blk = MOBA_BLOCK
    nb = k_ref.shape[0] // blk
    b31 = b31_ref[pl.program_id(1)]

    _fill_values(v_ref, vt_ref, blk)
    means = [jnp.mean(k_ref[n * blk:(n + 1) * blk, :].astype(F32), axis=0, keepdims=True)
             for n in range(nb)]
    km = jnp.concatenate(means, axis=0)
    km_hi = km.astype(BF16)
    km_lo = (km - km_hi.astype(F32)).astype(BF16)
    row = lax.broadcasted_iota(jnp.int32, (nb, blk), 0).astype(F32)
    bvec = jnp.broadcast_to(bvec_ref[...] * LOG2E, (blk, 2 * blk))
    toep = pltpu.roll(bvec, 0, 1, stride=1, stride_axis=0)
    krow = lax.broadcasted_iota(jnp.int32, (blk, blk), 0)
    qcol = lax.broadcasted_iota(jnp.int32, (blk, blk), 1)
    town = jnp.where(qcol >= krow, toep[:, 0:blk], NEG)
    tadj = toep[:, blk:2 * blk]

    def prep(i):
        q = q_ref[i * blk:(i + 1) * blk, :]
        gate = _dot_nt(km_hi, q) + _dot_nt(km_lo, q)
        valid = row < float(i)
        gm = jnp.where(valid, gate, -jnp.inf)
        sel = jnp.full(gate.shape, NEG, F32)
        for _ in range(min(MOBA_TOPK, nb)):
            mx = jnp.max(gm, axis=0, keepdims=True)
            first = jnp.min(jnp.where(gm == mx, row, float(nb)), axis=0, keepdims=True)
            pick = row == first
            sel = jnp.where(jnp.logical_and(pick, valid), 0.0, sel)
            gm = jnp.where(pick, -jnp.inf, gm)

        qs = (q.astype(F32) * (HEAD_DIM ** -0.5 * LOG2E)).astype(BF16)
        return qs, (sel, sel + b31 * LOG2E)

    def bias(i, j, aux):
        sel, far = aux
        if j == i:
            return town
        if j == i - 1:
            return tadj + sel[j:j + 1, :]
        return far[j:j + 1, :]

    _block_attention(nb, blk, MOBA_ISSUE, k_ref, vt_ref, s_ref, o_ref, prep, bias)


def _moba(proj, b31, bvec, batch, seq, col0):
    t = proj.shape[0]
    nh = MOBA_HEADS
    blk = MOBA_BLOCK

    def head_spec(group):
        return pl.BlockSpec((seq, HEAD_DIM), lambda b, h: (b, col0 + group * nh + h))

    return pl.pallas_call(
        _moba_kernel,
        grid=(batch, nh),
        in_specs=[pl.BlockSpec(memory_space=pltpu.SMEM),
                  head_spec(0), head_spec(1), head_spec(2),
                  pl.BlockSpec((None, 1, 2 * blk), lambda b, h: (h, 0, 0))],
        out_specs=pl.BlockSpec((seq, HEAD_DIM), lambda b, h: (b, h)),
        out_shape=jax.ShapeDtypeStruct((t, nh * HEAD_DIM), BF16),
        scratch_shapes=[pltpu.VMEM((HEAD_DIM + BF16_SUBLANES, seq), BF16),
                        pltpu.VMEM((2, seq, blk), F32)],
        compiler_params=_params("parallel", "parallel"),
        name="moba",
    )(b31, proj, proj, proj, bvec)


def _fox_gate_kernel(fl_ref, bf_ref, o_ref):
    x = fl_ref[...] + bf_ref[...]
    y = jnp.minimum(x, 0.0) - jnp.log1p(jnp.exp(-jnp.abs(x)))
    n = y.shape[1]
    lane = lax.broadcasted_iota(jnp.int32, y.shape, 1)
    sh = 1
    while sh < n:
        y = y + jnp.where(lane >= sh, pltpu.roll(y, sh, 1), 0.0)
        sh *= 2
    o_ref[...] = y


def _fox_gate(fl_t, b_forget, batch, seq):
    nh = fl_t.shape[0]
    return pl.pallas_call(
        _fox_gate_kernel,
        grid=(batch,),
        in_specs=[pl.BlockSpec((nh, seq), lambda b: (0, b)), _resident((nh, 1))],
        out_specs=pl.BlockSpec((nh, seq), lambda b: (0, b)),
        out_shape=jax.ShapeDtypeStruct((nh, batch * seq), F32),
        compiler_params=_params("parallel"),
        name="fox_gate",
    )(fl_t, b_forget)


def _fox_kernel(q_ref, k_ref, v_ref, f_ref, o_ref, ka_ref, vt_ref, s_ref):
    blk = FOX_BLOCK
    nb = k_ref.shape[0] // blk
    d = HEAD_DIM
    frow = f_ref[pl.ds(pl.program_id(1), 1), :] * (-LOG2E)
    hi = frow.astype(BF16).astype(F32)
    mid = (frow - hi).astype(BF16).astype(F32)
    lo = frow - hi - mid
    sub = lax.broadcasted_iota(jnp.int32, (8, frow.shape[1]), 0)
    top = jnp.where(sub == 0, hi, jnp.where(sub == 1, mid, jnp.where(sub == 2, lo, 0.0)))
    pad = jnp.zeros((d - 8, blk), F32)
    lane = lax.broadcasted_iota(jnp.int32, (blk, d), 1)
    for n in range(nb):
        rows = slice(n * blk, (n + 1) * blk)
        aug = jnp.transpose(jnp.concatenate([top[:, rows], pad], axis=0))
        ka_ref[rows, 0:d] = k_ref[rows, :]
        ka_ref[rows, d:2 * d] = aug.astype(BF16)
    _fill_values(v_ref, vt_ref, blk)
    ones = jnp.where(lane < 3, 1.0, 0.0).astype(BF16)
    krow = lax.broadcasted_iota(jnp.int32, (blk, blk), 0)
    qcol = lax.broadcasted_iota(jnp.int32, (blk, blk), 1)
    causal = jnp.where(qcol >= krow, 0.0, NEG)

    def prep(i):
        q = q_ref[i * blk:(i + 1) * blk, :].astype(F32)
        return jnp.concatenate([(q * (d ** -0.5 * LOG2E)).astype(BF16), ones], axis=1), None

    def bias(i, j, aux):
        return causal if j == i else None

    _block_attention(nb, blk, FOX_ISSUE, ka_ref, vt_ref, s_ref, o_ref, prep, bias)


def _fox(qkv, f, batch, seq):
    t = qkv.shape[0]
    nh = FOX_HEADS
    blk = FOX_BLOCK

    def head_spec(group):
        return pl.BlockSpec((seq, HEAD_DIM), lambda b, h: (b, group * nh + h))

    return pl.pallas_call(
        _fox_kernel,
        grid=(batch, nh),
        in_specs=[head_spec(0), head_spec(1), head_spec(2),
                  pl.BlockSpec((nh, seq), lambda b, h: (0, b))],
        out_specs=pl.BlockSpec((seq, HEAD_DIM), lambda b, h: (b, h)),
        out_shape=jax.ShapeDtypeStruct((t, nh * HEAD_DIM), BF16),
        scratch_shapes=[pltpu.VMEM((seq, 2 * HEAD_DIM), BF16),
                        pltpu.VMEM((HEAD_DIM + BF16_SUBLANES, seq), BF16),
                        pltpu.VMEM((2, seq, blk), F32)],
        compiler_params=_params("parallel", "parallel"),
        name="fox",
    )(qkv, qkv, qkv, f)


def _rotary_tables(seq):
    inv_freq = jnp.power(ROPE_BASE, -jnp.arange(0, HEAD_DIM, 2, dtype=F32) / HEAD_DIM)
    ang = jnp.arange(seq, dtype=F32)[:, None] * inv_freq[None, :]
    cos, sin = jnp.cos(ang), jnp.sin(ang)
    return jnp.concatenate([cos, cos], axis=-1), jnp.concatenate([-sin, sin], axis=-1)


def _t5_bucket(rel):
    n = jnp.maximum(rel, 0)
    max_exact = N_REL_BUCKETS // 2
    nf = jnp.maximum(n, 1).astype(F32)
    large = max_exact + (jnp.log(nf / max_exact) / math.log(REL_MAX_DISTANCE / max_exact)
                         * (N_REL_BUCKETS - max_exact)).astype(jnp.int32)
    large = jnp.minimum(large, N_REL_BUCKETS - 1)
    return jnp.where(n < max_exact, n, large)


def _moba_bias_vectors(rel_bias):
    blk = MOBA_BLOCK
    bias_t = rel_bias.astype(F32).T
    hot = _t5_bucket(jnp.arange(2 * blk))[:, None] == jnp.arange(N_REL_BUCKETS)
    near = jnp.sum(jnp.where(hot[None], bias_t[:, None, :], 0.0), axis=-1)
    far = bias_t[:, _t5_bucket(jnp.array(2 * blk, jnp.int32))]
    return far, near[:, None, :]


def kernel(x, rel_bias, norm_mix_pre, norm_mix_post, norm_ffn_pre, norm_ffn_post,
           even_w_in, even_gn, even_w_out, odd_w_in, odd_b_forget, odd_w_out,
           ffn_w_in, ffn_w_out):
    batch, seq, d = x.shape
    depth = norm_mix_pre.shape[0]
    xt = x.reshape(batch * seq, d)

    cos2, sin2 = _rotary_tables(seq)
    log_g = jnp.log1p(-jnp.power(2.0, -5.0 - jnp.arange(RET_HEADS, dtype=F32)))
    b31, bvec = _moba_bias_vectors(rel_bias)

    def row(v):
        return v.reshape(1, -1).astype(F32)

    for layer in range(depth):
        i = layer // 2
        if layer % 2 == 0:
            proj = _norm_proj(xt, row(norm_mix_pre[layer]), even_w_in[i].astype(BF16))
            ret = _retention(proj, log_g, cos2, sin2, row(even_gn[i]), batch, seq)
            mob = _moba(proj, b31, bvec, batch, seq, 4 * RET_HEADS)
            acts, w_mix = [ret, mob], even_w_out[i]
        else:
            w_in = jnp.pad(odd_w_in[i], ((0, 0), (0, HEAD_DIM - FOX_HEADS))).astype(BF16)
            qkv, fl_t = _norm_proj_gate(xt, row(norm_mix_pre[layer]), w_in, FOX_HEADS)
            f = _fox_gate(fl_t, odd_b_forget[i].reshape(-1, 1).astype(F32), batch, seq)
            acts, w_mix = [_fox(qkv, f, batch, seq)], odd_w_out[i]
        xt = _mix_ffn(acts, w_mix.astype(BF16), row(norm_mix_post[layer]), xt,
                      row(norm_ffn_pre[layer]), ffn_w_in[layer].astype(BF16),
                      ffn_w_out[layer].astype(BF16), row(norm_ffn_post[layer]))
    return xt.reshape(batch, seq, d)
```

```python
import functools
import math

import jax
import jax.numpy as jnp
from jax import lax
from jax.experimental import pallas as pl
from jax.experimental.pallas import tpu as pltpu

F32 = jnp.float32
BF16 = jnp.bfloat16

HEAD_DIM = 128
RET_HEADS = 4
MOBA_HEADS = 4
FOX_HEADS = 8
RET_CHUNK = 128
MOBA_BLOCK = 256
MOBA_TOPK = 3
N_REL_BUCKETS = 32
REL_MAX_DISTANCE = 128
ROPE_BASE = 10000.0
RMS_EPS = 1e-6
GN_EPS = 1e-5
NEG = -1e30
LOG2E = math.log2(math.e)

V7X_VMEM_BYTES = 64 * 1024 * 1024
VMEM_LIMIT = V7X_VMEM_BYTES - 8 * 1024 * 1024

ROW_TILE = 1024
MIX_ROWS = 1024
SUB_ROWS = 512
COL_CHUNK = 512
FFN_CHUNK = 256
FOX_BLOCK = 256
BF16_SUBLANES = 16
MOBA_ISSUE = (2, 3)
RET_FILL_EVERY = 1
FOX_ISSUE = (4, 2)


def _params(*sem, flags=None):
    return pltpu.CompilerParams(dimension_semantics=sem, vmem_limit_bytes=VMEM_LIMIT,
                                flags=flags)


def _resident(shape):
    nd = len(shape)
    return pl.BlockSpec(shape, lambda *_: (0,) * nd, pipeline_mode=pl.Buffered(1))


def _rms(x, g):
    return x * lax.rsqrt(jnp.mean(x * x, axis=-1, keepdims=True) + RMS_EPS) * g


def _dot(a, b):
    return jnp.dot(a, b, preferred_element_type=F32)


def _dot_nt(a, b):
    return lax.dot_general(a, b, (((1,), (1,)), ((), ())), preferred_element_type=F32)


def _sub_tiles(rows, sub):
    return [slice(r, r + sub) for r in range(0, rows, sub)]


def _norm_proj_kernel(x_ref, g_ref, w_ref, o_ref):
    subs = _sub_tiles(x_ref.shape[0], SUB_ROWS)
    hns = [_rms(x_ref[rows, :], g_ref[...]).astype(BF16) for rows in subs]
    for rows, hn in zip(subs, hns):
        for c in range(o_ref.shape[1] // COL_CHUNK):
            sl = slice(c * COL_CHUNK, (c + 1) * COL_CHUNK)
            o_ref[rows, sl] = _dot(hn, w_ref[:, sl]).astype(o_ref.dtype)


def _norm_proj(x, g, w):
    t, d = x.shape
    n = w.shape[1]
    return pl.pallas_call(
        _norm_proj_kernel,
        grid=(t // ROW_TILE,),
        in_specs=[pl.BlockSpec((ROW_TILE, d), lambda i: (i, 0)),
                  _resident((1, d)), _resident((d, n))],
        out_specs=pl.BlockSpec((ROW_TILE, n), lambda i: (i, 0)),
        out_shape=jax.ShapeDtypeStruct((t, n), BF16),
        compiler_params=_params("parallel"),
        name="norm_proj",
    )(x, g, w)


def _norm_proj_gate_kernel(x_ref, g_ref, w_ref, o_ref, f_ref):
    n = o_ref.shape[1]
    subs = _sub_tiles(x_ref.shape[0], SUB_ROWS)
    hns = [_rms(x_ref[rows, :], g_ref[...]).astype(BF16) for rows in subs]
    for rows, hn in zip(subs, hns):
        for c in range(n // COL_CHUNK):
            sl = slice(c * COL_CHUNK, (c + 1) * COL_CHUNK)
            o_ref[rows, sl] = _dot(hn, w_ref[:, sl]).astype(o_ref.dtype)
        fl = _dot(hn, w_ref[:, n:])
        f_ref[:, rows] = jnp.transpose(fl)[0:f_ref.shape[0], :]


def _norm_proj_gate(x, g, w, nh):
    t, d = x.shape
    n = w.shape[1] - HEAD_DIM
    return pl.pallas_call(
        _norm_proj_gate_kernel,
        grid=(t // ROW_TILE,),
        in_specs=[pl.BlockSpec((ROW_TILE, d), lambda i: (i, 0)),
                  _resident((1, d)), _resident(w.shape)],
        out_specs=[pl.BlockSpec((ROW_TILE, n), lambda i: (i, 0)),
                   pl.BlockSpec((nh, ROW_TILE), lambda i: (0, i))],
        out_shape=[jax.ShapeDtypeStruct((t, n), BF16),
                   jax.ShapeDtypeStruct((nh, t), F32)],
        compiler_params=_params("parallel"),
        name="norm_proj_gate",
    )(x, g, w)


def _mix_ffn_kernel(n_in, *refs):
    a_refs = refs[:n_in]
    wmix_ref, gmix_ref, x_ref, gpre_ref, wffn_ref, wo_ref, gpost_ref, o_ref = refs[n_in:]
    hidden = wo_ref.shape[0]
    subs = _sub_tiles(x_ref.shape[0], SUB_ROWS)

    xs, hns = [], []
    for rows in subs:
        m = None
        off = 0
        for a_ref in a_refs:
            width = a_ref.shape[1]
            part = _dot(a_ref[rows, :], wmix_ref[off:off + width, :])
            m = part if m is None else m + part
            off += width
        x = x_ref[rows, :] + _rms(m, gmix_ref[...])
        xs.append(x)
        hns.append(_rms(x, gpre_ref[...]).astype(BF16))
    for rows, x, hn in zip(subs, xs, hns):
        y = jnp.zeros(x.shape, F32)
        for c in range(hidden // FFN_CHUNK):
            lo = c * FFN_CHUNK
            gate = _dot(hn, wffn_ref[:, lo:lo + FFN_CHUNK])
            up = _dot(hn, wffn_ref[:, hidden + lo:hidden + lo + FFN_CHUNK])
            act = (gate * jax.nn.sigmoid(gate) * up).astype(BF16)
            y = y + _dot(act, wo_ref[lo:lo + FFN_CHUNK, :])
        o_ref[rows, :] = x + _rms(y, gpost_ref[...])


def _mix_ffn(acts, w_mix, g_mix, x, g_pre, w_ffn, w_o, g_post):
    t, d = x.shape
    in_specs = [pl.BlockSpec((MIX_ROWS, a.shape[1]), lambda i: (i, 0)) for a in acts]
    in_specs += [_resident(w_mix.shape), _resident((1, d)),
                 pl.BlockSpec((MIX_ROWS, d), lambda i: (i, 0)),
                 _resident((1, d)), _resident(w_ffn.shape), _resident(w_o.shape),
                 _resident((1, d))]
    return pl.pallas_call(
        functools.partial(_mix_ffn_kernel, len(acts)),
        grid=(t // MIX_ROWS,),
        in_specs=in_specs,
        out_specs=pl.BlockSpec((MIX_ROWS, d), lambda i: (i, 0)),
        out_shape=jax.ShapeDtypeStruct((t, d), F32),
        compiler_params=_params("parallel"),
        name="mix_ffn",
    )(*acts, w_mix, g_mix, x, g_pre, w_ffn, w_o, g_post)


def _retention_work(lg, q_ref, k_ref, v_ref, g_ref, cos_ref, sin_ref, gn_ref, o_ref):
    c = RET_CHUNK
    seq = q_ref.shape[0]
    row = lax.broadcasted_iota(jnp.int32, (c, c), 0).astype(F32)
    col = lax.broadcasted_iota(jnp.int32, (c, c), 1).astype(F32)
    diff = row - col
    inner = jnp.where(diff >= 0, jnp.exp(lg * jnp.maximum(diff, 0.0)), 0.0)
    idx = lax.broadcasted_iota(jnp.int32, (c, HEAD_DIM), 0).astype(F32)
    q_dec = jnp.exp(lg * (idx + 1.0))
    k_dec = jnp.exp(lg * (c - 1.0 - idx))
    chunk_dec = jnp.exp(lg * jnp.full((1, HEAD_DIM), float(c), F32))
    gn_w = gn_ref[...]
    scale = HEAD_DIM ** -0.5

    def rot(t, cos, sin):
        return t * cos + pltpu.roll(t, HEAD_DIM // 2, 1) * sin

    def head(n):
        sl = slice(n * c, (n + 1) * c)
        cos = cos_ref[sl, :]
        sin = sin_ref[sl, :]
        q = rot(q_ref[sl, :].astype(F32), cos, sin) * scale
        k = rot(k_ref[sl, :].astype(F32), cos, sin)
        qb = q.astype(BF16)
        return qb, k, _dot_nt(qb, k.astype(BF16))

    def tail(n, state, qb, k, sc):
        sl = slice(n * c, (n + 1) * c)
        v = v_ref[sl, :]
        kd_t = jnp.transpose(k * k_dec).astype(BF16)
        kv = _dot(kd_t, v)
        cross = _dot(qb, state.astype(BF16))
        o = _dot((sc * inner).astype(BF16), v) + cross * q_dec
        mu = jnp.mean(o, axis=-1, keepdims=True)
        var = jnp.mean(jnp.square(o - mu), axis=-1, keepdims=True)
        y = (o - mu) * lax.rsqrt(var + GN_EPS) * gn_w
        gate = g_ref[sl, :].astype(F32)
        o_ref[sl, :] = (y * (gate * jax.nn.sigmoid(gate))).astype(o_ref.dtype)
        return state * chunk_dec + kv

    n_chunks = seq // c
    work = {"n": 0, "state": jnp.zeros((HEAD_DIM, HEAD_DIM), F32), "ahead": head(0)}

    def emit():
        n = work["n"]
        if n >= n_chunks:
            return False
        cur = work["ahead"]
        if n + 1 < n_chunks:
            work["ahead"] = head(n + 1)
        work["state"] = tail(n, work["state"], *cur)
        work["n"] = n + 1
        return n + 1 < n_chunks

    return emit


def _group_reduce(x, op):
    r, c = x.shape
    g = r // 8
    if g % 4 == 0:
        return op(op(x.reshape(4, g // 4, 8, c), axis=1), axis=0)
    return op(x.reshape(g, 8, c), axis=0)


def _block_attention(nb, blk, issue, keys_ref, vt_ref, s_ref, o_ref, prep, bias,
                     filler=None, fill_every=1):
    d = HEAD_DIM
    chunk, lag = issue
    steps = 0

    def chunks(i):
        return [(lo, min(lo + chunk, i + 1)) for lo in range(0, i + 1, chunk)]

    def score_chunk(i, lo, hi, qa, aux, m8):
        s = _dot_nt(keys_ref[lo * blk:hi * blk, :], qa)
        for j in range(lo, hi):
            t = s[(j - lo) * blk:(j - lo + 1) * blk, :]
            b = bias(i, j, aux)
            if b is not None:
                t = t + b
            s_ref[i % 2, j * blk:(j + 1) * blk, :] = t
            t8 = _group_reduce(t, jnp.max)
            m8 = t8 if m8 is None else jnp.maximum(m8, t8)
        return m8

    def attend_chunk(i, lo, hi, m, acc):
        ps = [jnp.exp2(s_ref[i % 2, j * blk:(j + 1) * blk, :] - m).astype(BF16)
              for j in range(lo, hi)]
        p = ps[0] if len(ps) == 1 else jnp.concatenate(ps, axis=0)
        part = _dot(vt_ref[:, lo * blk:hi * blk], p)
        return part if acc is None else acc + part

    qa, aux = prep(nb - 1)
    m8 = None
    for lo, hi in chunks(nb - 1):
        m8 = score_chunk(nb - 1, lo, hi, qa, aux, m8)
    for i in reversed(range(nb)):
        m = jnp.max(m8, axis=0, keepdims=True)
        if i > 0:
            qa, aux = prep(i - 1)
        ahead = chunks(i - 1) if i > 0 else []
        here = chunks(i)
        m8 = None
        acc = None
        snaps = []
        for c in range(len(here)):
            snap = snaps[-lag] if len(snaps) >= lag else None
            m_c = m if snap is None else jnp.minimum(m, jnp.maximum(m, snap[0:1, :]))
            if c < len(ahead):
                m8 = score_chunk(i - 1, *ahead[c], qa, aux, m8)
                snaps.append(m8)
            acc = attend_chunk(i, *here[c], m_c, acc)
            steps += 1
            if filler is not None and steps % fill_every == 0:
                filler()
        out = acc[0:d, :] / acc[d:d + 1, :]
        o_ref[i * blk:(i + 1) * blk, :] = jnp.transpose(out).astype(o_ref.dtype)
    while filler is not None and filler():
        pass


def _fill_values(v_ref, vt_ref, blk):
    d = HEAD_DIM
    seq = v_ref.shape[0]
    for n in range(seq // blk):
        rows = slice(n * blk, (n + 1) * blk)
        vt_ref[0:d, rows] = jnp.transpose(v_ref[rows, :].astype(F32)).astype(BF16)
    first = lax.broadcasted_iota(jnp.int32, (vt_ref.shape[0] - d, seq), 0) == 0
    vt_ref[d:, :] = jnp.where(first, 1.0, 0.0).astype(BF16)


def _even_mixer_kernel(lg_ref, b31_ref, rq_ref, rk_ref, rv_ref, rg_ref, cos_ref, sin_ref,
                       gn_ref, q_ref, k_ref, v_ref, bvec_ref, ret_ref, o_ref, vt_ref, s_ref):
    blk = MOBA_BLOCK
    nb = k_ref.shape[0] // blk
    h = pl.program_id(1)
    b31 = b31_ref[h]
    retention_chunk = _retention_work(lg_ref[h], rq_ref, rk_ref, rv_ref, rg_ref,
                                      cos_ref, sin_ref, gn_ref, ret_ref)

    _fill_values(v_ref, vt_ref, blk)
    means = [jnp.mean(k_ref[n * blk:(n + 1) * blk, :].astype(F32), axis=0, keepdims=True)
             for n in range(nb)]
    km = jnp.concatenate(means, axis=0)
    km_hi = km.astype(BF16)
    km_lo = (km - km_hi.astype(F32)).astype(BF16)
    row = lax.broadcasted_iota(jnp.int32, (nb, blk), 0).astype(F32)
    bvec = jnp.broadcast_to(bvec_ref[...] * LOG2E, (blk, 2 * blk))
    toep = pltpu.roll(bvec, 0, 1, stride=1, stride_axis=0)
    krow = lax.broadcasted_iota(jnp.int32, (blk, blk), 0)
    qcol = lax.broadcasted_iota(jnp.int32, (blk, blk), 1)
    town = jnp.where(qcol >= krow, toep[:, 0:blk], NEG)
    tadj = toep[:, blk:2 * blk]

    def prep(i):
        q = q_ref[i * blk:(i + 1) * blk, :]
        gate = _dot_nt(km_hi, q) + _dot_nt(km_lo, q)
        valid = row < float(i)
        gm = jnp.where(valid, gate, -jnp.inf)
        sel = jnp.full(gate.shape, NEG, F32)
        for _ in range(min(MOBA_TOPK, nb)):
            mx = jnp.max(gm, axis=0, keepdims=True)
            first = jnp.min(jnp.where(gm == mx, row, float(nb)), axis=0, keepdims=True)
            pick = row == first
            sel = jnp.where(jnp.logical_and(pick, valid), 0.0, sel)
            gm = jnp.where(pick, -jnp.inf, gm)

        qs = (q.astype(F32) * (HEAD_DIM ** -0.5 * LOG2E)).astype(BF16)
        return qs, (sel, sel + b31 * LOG2E)

    def bias(i, j, aux):
        sel, far = aux
        if j == i:
            return town
        if j == i - 1:
            return tadj + sel[j:j + 1, :]
        return far[j:j + 1, :]

    _block_attention(nb, blk, MOBA_ISSUE, k_ref, vt_ref, s_ref, o_ref, prep, bias,
                     filler=retention_chunk, fill_every=RET_FILL_EVERY)


def _even_mixer(proj, log_g, cos2, sin2, gn_w, b31, bvec, batch, seq):
    assert RET_HEADS == MOBA_HEADS
    t = proj.shape[0]
    nh = MOBA_HEADS
    blk = MOBA_BLOCK

    def head_spec(group):
        return pl.BlockSpec((seq, HEAD_DIM), lambda b, h: (b, group * nh + h))

    out_spec = pl.BlockSpec((seq, HEAD_DIM), lambda b, h: (b, h))
    out_shape = jax.ShapeDtypeStruct((t, nh * HEAD_DIM), BF16)
    return pl.pallas_call(
        _even_mixer_kernel,
        grid=(batch, nh),
        in_specs=[pl.BlockSpec(memory_space=pltpu.SMEM), pl.BlockSpec(memory_space=pltpu.SMEM),
                  head_spec(0), head_spec(1), head_spec(2), head_spec(3),
                  _resident((seq, HEAD_DIM)), _resident((seq, HEAD_DIM)),
                  pl.BlockSpec((1, HEAD_DIM), lambda b, h: (0, h)),
                  head_spec(4), head_spec(5), head_spec(6),
                  pl.BlockSpec((None, 1, 2 * blk), lambda b, h: (h, 0, 0))],
        out_specs=[out_spec, out_spec],
        out_shape=[out_shape, out_shape],
        scratch_shapes=[pltpu.VMEM((HEAD_DIM + BF16_SUBLANES, seq), BF16),
                        pltpu.VMEM((2, seq, blk), F32)],
        compiler_params=_params("parallel", "parallel"),
        name="even_mixer",
    )(log_g, b31, proj, proj, proj, proj, cos2, sin2, gn_w, proj, proj, proj, bvec)


def _fox_gate_kernel(fl_ref, bf_ref, o_ref):
    x = fl_ref[...] + bf_ref[...]
    y = jnp.minimum(x, 0.0) - jnp.log1p(jnp.exp(-jnp.abs(x)))
    n = y.shape[1]
    lane = lax.broadcasted_iota(jnp.int32, y.shape, 1)
    sh = 1
    while sh < n:
        y = y + jnp.where(lane >= sh, pltpu.roll(y, sh, 1), 0.0)
        sh *= 2
    o_ref[...] = y


def _fox_gate(fl_t, b_forget, batch, seq):
    nh = fl_t.shape[0]
    return pl.pallas_call(
        _fox_gate_kernel,
        grid=(batch,),
        in_specs=[pl.BlockSpec((nh, seq), lambda b: (0, b)), _resident((nh, 1))],
        out_specs=pl.BlockSpec((nh, seq), lambda b: (0, b)),
        out_shape=jax.ShapeDtypeStruct((nh, batch * seq), F32),
        compiler_params=_params("parallel"),
        name="fox_gate",
    )(fl_t, b_forget)


def _fox_kernel(q_ref, k_ref, v_ref, f_ref, o_ref, ka_ref, vt_ref, s_ref):
    blk = FOX_BLOCK
    nb = k_ref.shape[0] // blk
    d = HEAD_DIM
    frow = f_ref[pl.ds(pl.program_id(1), 1), :] * (-LOG2E)
    hi = frow.astype(BF16).astype(F32)
    mid = (frow - hi).astype(BF16).astype(F32)
    lo = frow - hi - mid
    sub = lax.broadcasted_iota(jnp.int32, (8, frow.shape[1]), 0)
    top = jnp.where(sub == 0, hi, jnp.where(sub == 1, mid, jnp.where(sub == 2, lo, 0.0)))
    pad = jnp.zeros((d - 8, blk), F32)
    lane = lax.broadcasted_iota(jnp.int32, (blk, d), 1)
    for n in range(nb):
        rows = slice(n * blk, (n + 1) * blk)
        aug = jnp.transpose(jnp.concatenate([top[:, rows], pad], axis=0))
        ka_ref[rows, 0:d] = k_ref[rows, :]
        ka_ref[rows, d:2 * d] = aug.astype(BF16)
    _fill_values(v_ref, vt_ref, blk)
    ones = jnp.where(lane < 3, 1.0, 0.0).astype(BF16)
    krow = lax.broadcasted_iota(jnp.int32, (blk, blk), 0)
    qcol = lax.broadcasted_iota(jnp.int32, (blk, blk), 1)
    causal = jnp.where(qcol >= krow, 0.0, NEG)

    def prep(i):
        q = q_ref[i * blk:(i + 1) * blk, :].astype(F32)
        return jnp.concatenate([(q * (d ** -0.5 * LOG2E)).astype(BF16), ones], axis=1), None

    def bias(i, j, aux):
        return causal if j == i else None

    _block_attention(nb, blk, FOX_ISSUE, ka_ref, vt_ref, s_ref, o_ref, prep, bias)


def _fox(qkv, f, batch, seq):
    t = qkv.shape[0]
    nh = FOX_HEADS
    blk = FOX_BLOCK

    def head_spec(group):
        return pl.BlockSpec((seq, HEAD_DIM), lambda b, h: (b, group * nh + h))

    return pl.pallas_call(
        _fox_kernel,
        grid=(batch, nh),
        in_specs=[head_spec(0), head_spec(1), head_spec(2),
                  pl.BlockSpec((nh, seq), lambda b, h: (0, b))],
        out_specs=pl.BlockSpec((seq, HEAD_DIM), lambda b, h: (b, h)),
        out_shape=jax.ShapeDtypeStruct((t, nh * HEAD_DIM), BF16),
        scratch_shapes=[pltpu.VMEM((seq, 2 * HEAD_DIM), BF16),
                        pltpu.VMEM((HEAD_DIM + BF16_SUBLANES, seq), BF16),
                        pltpu.VMEM((2, seq, blk), F32)],
        compiler_params=_params("parallel", "parallel"),
        name="fox",
    )(qkv, qkv, qkv, f)


def _rotary_tables(seq):
    inv_freq = jnp.power(ROPE_BASE, -jnp.arange(0, HEAD_DIM, 2, dtype=F32) / HEAD_DIM)
    ang = jnp.arange(seq, dtype=F32)[:, None] * inv_freq[None, :]
    cos, sin = jnp.cos(ang), jnp.sin(ang)
    return jnp.concatenate([cos, cos], axis=-1), jnp.concatenate([-sin, sin], axis=-1)


def _t5_bucket(rel):
    n = jnp.maximum(rel, 0)
    max_exact = N_REL_BUCKETS // 2
    nf = jnp.maximum(n, 1).astype(F32)
    large = max_exact + (jnp.log(nf / max_exact) / math.log(REL_MAX_DISTANCE / max_exact)
                         * (N_REL_BUCKETS - max_exact)).astype(jnp.int32)
    large = jnp.minimum(large, N_REL_BUCKETS - 1)
    return jnp.where(n < max_exact, n, large)


def _moba_bias_vectors(rel_bias):
    blk = MOBA_BLOCK
    bias_t = rel_bias.astype(F32).T
    hot = _t5_bucket(jnp.arange(2 * blk))[:, None] == jnp.arange(N_REL_BUCKETS)
    near = jnp.sum(jnp.where(hot[None], bias_t[:, None, :], 0.0), axis=-1)
    far = bias_t[:, _t5_bucket(jnp.array(2 * blk, jnp.int32))]
    return far, near[:, None, :]


def kernel(x, rel_bias, norm_mix_pre, norm_mix_post, norm_ffn_pre, norm_ffn_post,
           even_w_in, even_gn, even_w_out, odd_w_in, odd_b_forget, odd_w_out,
           ffn_w_in, ffn_w_out):
    batch, seq, d = x.shape
    depth = norm_mix_pre.shape[0]
    xt = x.reshape(batch * seq, d)

    cos2, sin2 = _rotary_tables(seq)
    log_g = jnp.log1p(-jnp.power(2.0, -5.0 - jnp.arange(RET_HEADS, dtype=F32)))
    b31, bvec = _moba_bias_vectors(rel_bias)

    def row(v):
        return v.reshape(1, -1).astype(F32)

    for layer in range(depth):
        i = layer // 2
        if layer % 2 == 0:
            proj = _norm_proj(xt, row(norm_mix_pre[layer]), even_w_in[i].astype(BF16))
            ret, mob = _even_mixer(proj, log_g, cos2, sin2, row(even_gn[i]), b31, bvec,
                                   batch, seq)
            acts, w_mix = [ret, mob], even_w_out[i]
        else:
            w_in = jnp.pad(odd_w_in[i], ((0, 0), (0, HEAD_DIM - FOX_HEADS))).astype(BF16)
            qkv, fl_t = _norm_proj_gate(xt, row(norm_mix_pre[layer]), w_in, FOX_HEADS)
            f = _fox_gate(fl_t, odd_b_forget[i].reshape(-1, 1).astype(F32), batch, seq)
            acts, w_mix = [_fox(qkv, f, batch, seq)], odd_w_out[i]
        xt = _mix_ffn(acts, w_mix.astype(BF16), row(norm_mix_post[layer]), xt,
                      row(norm_ffn_pre[layer]), ffn_w_in[layer].astype(BF16),
                      ffn_w_out[layer].astype(BF16), row(norm_ffn_post[layer]))
    return xt.reshape(batch, seq, d)
```

```python
import functools
import math

import jax
import jax.numpy as jnp
from jax import lax
from jax.experimental import pallas as pl
from jax.experimental.pallas import tpu as pltpu

F32 = jnp.float32
BF16 = jnp.bfloat16

HEAD_DIM = 128
RET_HEADS = 4
MOBA_HEADS = 4
FOX_HEADS = 8
RET_CHUNK = 128
MOBA_BLOCK = 256
MOBA_TOPK = 3
N_REL_BUCKETS = 32
REL_MAX_DISTANCE = 128
ROPE_BASE = 10000.0
RMS_EPS = 1e-6
GN_EPS = 1e-5
NEG = -1e30
LOG2E = math.log2(math.e)

V7X_VMEM_BYTES = 64 * 1024 * 1024
VMEM_LIMIT = V7X_VMEM_BYTES - 8 * 1024 * 1024

ROW_TILE = 1024
MIX_ROWS = 1024
SUB_ROWS = 512
COL_CHUNK = 512
FFN_CHUNK = 256
FOX_BLOCK = 256
BF16_SUBLANES = 16
MOBA_ISSUE = (2, 5)
RET_FILL_EVERY = 1
FOX_ISSUE = (4, 4)


def _params(*sem, flags=None):
    return pltpu.CompilerParams(dimension_semantics=sem, vmem_limit_bytes=VMEM_LIMIT,
                                flags=flags)


def _resident(shape):
    nd = len(shape)
    return pl.BlockSpec(shape, lambda *_: (0,) * nd, pipeline_mode=pl.Buffered(1))


def _rms(x, g):
    return x * lax.rsqrt(jnp.mean(x * x, axis=-1, keepdims=True) + RMS_EPS) * g


def _dot(a, b):
    return jnp.dot(a, b, preferred_element_type=F32)


def _dot_nt(a, b):
    return lax.dot_general(a, b, (((1,), (1,)), ((), ())), preferred_element_type=F32)


def _sub_tiles(rows, sub):
    return [slice(r, r + sub) for r in range(0, rows, sub)]


def _norm_proj_kernel(x_ref, g_ref, w_ref, o_ref):
    subs = _sub_tiles(x_ref.shape[0], SUB_ROWS)
    hns = [_rms(x_ref[rows, :], g_ref[...]).astype(BF16) for rows in subs]
    for rows, hn in zip(subs, hns):
        for c in range(o_ref.shape[1] // COL_CHUNK):
            sl = slice(c * COL_CHUNK, (c + 1) * COL_CHUNK)
            o_ref[rows, sl] = _dot(hn, w_ref[:, sl]).astype(o_ref.dtype)


def _norm_proj(x, g, w):
    t, d = x.shape
    n = w.shape[1]
    return pl.pallas_call(
        _norm_proj_kernel,
        grid=(t // ROW_TILE,),
        in_specs=[pl.BlockSpec((ROW_TILE, d), lambda i: (i, 0)),
                  _resident((1, d)), _resident((d, n))],
        out_specs=pl.BlockSpec((ROW_TILE, n), lambda i: (i, 0)),
        out_shape=jax.ShapeDtypeStruct((t, n), BF16),
        compiler_params=_params("parallel"),
        name="norm_proj",
    )(x, g, w)


def _norm_proj_gate_kernel(x_ref, g_ref, w_ref, o_ref, f_ref):
    n = o_ref.shape[1]
    subs = _sub_tiles(x_ref.shape[0], SUB_ROWS)
    hns = [_rms(x_ref[rows, :], g_ref[...]).astype(BF16) for rows in subs]
    for rows, hn in zip(subs, hns):
        for c in range(n // COL_CHUNK):
            sl = slice(c * COL_CHUNK, (c + 1) * COL_CHUNK)
            o_ref[rows, sl] = _dot(hn, w_ref[:, sl]).astype(o_ref.dtype)
        fl = _dot(hn, w_ref[:, n:])
        f_ref[:, rows] = jnp.transpose(fl)[0:f_ref.shape[0], :]


def _norm_proj_gate(x, g, w, nh):
    t, d = x.shape
    n = w.shape[1] - HEAD_DIM
    return pl.pallas_call(
        _norm_proj_gate_kernel,
        grid=(t // ROW_TILE,),
        in_specs=[pl.BlockSpec((ROW_TILE, d), lambda i: (i, 0)),
                  _resident((1, d)), _resident(w.shape)],
        out_specs=[pl.BlockSpec((ROW_TILE, n), lambda i: (i, 0)),
                   pl.BlockSpec((nh, ROW_TILE), lambda i: (0, i))],
        out_shape=[jax.ShapeDtypeStruct((t, n), BF16),
                   jax.ShapeDtypeStruct((nh, t), F32)],
        compiler_params=_params("parallel"),
        name="norm_proj_gate",
    )(x, g, w)


def _mix_ffn_kernel(n_in, *refs):
    a_refs = refs[:n_in]
    wmix_ref, gmix_ref, x_ref, gpre_ref, wffn_ref, wo_ref, gpost_ref, o_ref = refs[n_in:]
    hidden = wo_ref.shape[0]
    subs = _sub_tiles(x_ref.shape[0], SUB_ROWS)

    xs, hns = [], []
    for rows in subs:
        m = None
        off = 0
        for a_ref in a_refs:
            width = a_ref.shape[1]
            part = _dot(a_ref[rows, :], wmix_ref[off:off + width, :])
            m = part if m is None else m + part
            off += width
        x = x_ref[rows, :] + _rms(m, gmix_ref[...])
        xs.append(x)
        hns.append(_rms(x, gpre_ref[...]).astype(BF16))
    for rows, x, hn in zip(subs, xs, hns):
        y = jnp.zeros(x.shape, F32)
        for c in range(hidden // FFN_CHUNK):
            lo = c * FFN_CHUNK
            gate = _dot(hn, wffn_ref[:, lo:lo + FFN_CHUNK])
            up = _dot(hn, wffn_ref[:, hidden + lo:hidden + lo + FFN_CHUNK])
            act = (gate * jax.nn.sigmoid(gate) * up).astype(BF16)
            y = y + _dot(act, wo_ref[lo:lo + FFN_CHUNK, :])
        o_ref[rows, :] = x + _rms(y, gpost_ref[...])


def _mix_ffn(acts, w_mix, g_mix, x, g_pre, w_ffn, w_o, g_post):
    t, d = x.shape
    in_specs = [pl.BlockSpec((MIX_ROWS, a.shape[1]), lambda i: (i, 0)) for a in acts]
    in_specs += [_resident(w_mix.shape), _resident((1, d)),
                 pl.BlockSpec((MIX_ROWS, d), lambda i: (i, 0)),
                 _resident((1, d)), _resident(w_ffn.shape), _resident(w_o.shape),
                 _resident((1, d))]
    return pl.pallas_call(
        functools.partial(_mix_ffn_kernel, len(acts)),
        grid=(t // MIX_ROWS,),
        in_specs=in_specs,
        out_specs=pl.BlockSpec((MIX_ROWS, d), lambda i: (i, 0)),
        out_shape=jax.ShapeDtypeStruct((t, d), F32),
        compiler_params=_params("parallel"),
        name="mix_ffn",
    )(*acts, w_mix, g_mix, x, g_pre, w_ffn, w_o, g_post)


def _retention_work(lg, q_ref, k_ref, v_ref, g_ref, cos_ref, sin_ref, gn_ref, o_ref):
    c = RET_CHUNK
    seq = q_ref.shape[0]
    row = lax.broadcasted_iota(jnp.int32, (c, c), 0).astype(F32)
    col = lax.broadcasted_iota(jnp.int32, (c, c), 1).astype(F32)
    diff = row - col
    inner = jnp.where(diff >= 0, jnp.exp(lg * jnp.maximum(diff, 0.0)), 0.0)
    idx = lax.broadcasted_iota(jnp.int32, (c, HEAD_DIM), 0).astype(F32)
    q_dec = jnp.exp(lg * (idx + 1.0))
    k_dec = jnp.exp(lg * (c - 1.0 - idx))
    chunk_dec = jnp.exp(lg * jnp.full((1, HEAD_DIM), float(c), F32))
    gn_w = gn_ref[...]
    scale = HEAD_DIM ** -0.5

    def rot(t, cos, sin):
        return t * cos + pltpu.roll(t, HEAD_DIM // 2, 1) * sin

    def head(n):
        sl = slice(n * c, (n + 1) * c)
        cos = cos_ref[sl, :]
        sin = sin_ref[sl, :]
        q = rot(q_ref[sl, :].astype(F32), cos, sin) * scale
        k = rot(k_ref[sl, :].astype(F32), cos, sin)
        qb = q.astype(BF16)
        return qb, k, _dot_nt(qb, k.astype(BF16))

    def tail(n, state, qb, k, sc):
        sl = slice(n * c, (n + 1) * c)
        v = v_ref[sl, :]
        kd_t = jnp.transpose(k * k_dec).astype(BF16)
        kv = _dot(kd_t, v)
        cross = _dot(qb, state.astype(BF16))
        o = _dot((sc * inner).astype(BF16), v) + cross * q_dec
        mu = jnp.mean(o, axis=-1, keepdims=True)
        var = jnp.mean(jnp.square(o - mu), axis=-1, keepdims=True)
        y = (o - mu) * lax.rsqrt(var + GN_EPS) * gn_w
        gate = g_ref[sl, :].astype(F32)
        o_ref[sl, :] = (y * (gate * jax.nn.sigmoid(gate))).astype(o_ref.dtype)
        return state * chunk_dec + kv

    n_chunks = seq // c
    work = {"n": 0, "state": jnp.zeros((HEAD_DIM, HEAD_DIM), F32), "ahead": head(0)}

    def emit():
        n = work["n"]
        if n >= n_chunks:
            return False
        cur = work["ahead"]
        if n + 1 < n_chunks:
            work["ahead"] = head(n + 1)
        work["state"] = tail(n, work["state"], *cur)
        work["n"] = n + 1
        return n + 1 < n_chunks

    return emit


def _group_reduce(x, op):
    r, c = x.shape
    g = r // 8
    if g % 4 == 0:
        return op(op(x.reshape(4, g // 4, 8, c), axis=1), axis=0)
    return op(x.reshape(g, 8, c), axis=0)


def _block_attention(nb, blk, issue, keys_ref, vt_ref, s_ref, o_ref, prep, bias,
                     filler=None, fill_every=1):
    d = HEAD_DIM
    chunk, lag = issue
    steps = 0

    def chunks(i):
        return [(lo, min(lo + chunk, i + 1)) for lo in range(0, i + 1, chunk)]

    def score_chunk(i, lo, hi, qa, aux, m8, snaps):
        s = _dot_nt(keys_ref[lo * blk:hi * blk, :], qa)
        for j in range(lo, hi):
            t = s[(j - lo) * blk:(j - lo + 1) * blk, :]
            b = bias(i, j, aux)
            if b is not None:
                t = t + b
            s_ref[i % 2, j * blk:(j + 1) * blk, :] = t
            t8 = _group_reduce(t, jnp.max)
            m8 = t8 if m8 is None else jnp.maximum(m8, t8)
            snaps[j] = m8
        return m8

    def attend_chunk(i, lo, hi, m, acc, snaps):
        ps = []
        for j in range(lo, hi):
            snap = snaps.get(j - lag)
            m_j = m if snap is None else jnp.minimum(m, jnp.maximum(m, snap[0:1, :]))
            ps.append(jnp.exp2(s_ref[i % 2, j * blk:(j + 1) * blk, :] - m_j).astype(BF16))
        p = ps[0] if len(ps) == 1 else jnp.concatenate(ps, axis=0)
        part = _dot(vt_ref[:, lo * blk:hi * blk], p)
        return part if acc is None else acc + part

    qa, aux = prep(nb - 1)
    m8 = None
    for lo, hi in chunks(nb - 1):
        m8 = score_chunk(nb - 1, lo, hi, qa, aux, m8, {})
    for i in reversed(range(nb)):
        m = jnp.max(m8, axis=0, keepdims=True)
        if i > 0:
            qa, aux = prep(i - 1)
        ahead = chunks(i - 1) if i > 0 else []
        here = chunks(i)
        m8 = None
        acc = None
        snaps = {}
        for c in range(len(here)):
            done = dict(snaps)
            if c < len(ahead):
                m8 = score_chunk(i - 1, *ahead[c], qa, aux, m8, snaps)
            acc = attend_chunk(i, *here[c], m, acc, done)
            steps += 1
            if filler is not None and steps % fill_every == 0:
                filler()
        out = acc[0:d, :] / acc[d:d + 1, :]
        o_ref[i * blk:(i + 1) * blk, :] = jnp.transpose(out).astype(o_ref.dtype)
    while filler is not None and filler():
        pass


def _fill_values(v_ref, vt_ref, blk):
    d = HEAD_DIM
    seq = v_ref.shape[0]
    for n in range(seq // blk):
        rows = slice(n * blk, (n + 1) * blk)
        vt_ref[0:d, rows] = jnp.transpose(v_ref[rows, :].astype(F32)).astype(BF16)
    first = lax.broadcasted_iota(jnp.int32, (vt_ref.shape[0] - d, seq), 0) == 0
    vt_ref[d:, :] = jnp.where(first, 1.0, 0.0).astype(BF16)


def _even_mixer_kernel(lg_ref, b31_ref, rq_ref, rk_ref, rv_ref, rg_ref, cos_ref, sin_ref,
                       gn_ref, q_ref, k_ref, v_ref, bvec_ref, ret_ref, o_ref, vt_ref, s_ref):
    blk = MOBA_BLOCK
    nb = k_ref.shape[0] // blk
    h = pl.program_id(1)
    b31 = b31_ref[h]
    retention_chunk = _retention_work(lg_ref[h], rq_ref, rk_ref, rv_ref, rg_ref,
                                      cos_ref, sin_ref, gn_ref, ret_ref)

    _fill_values(v_ref, vt_ref, blk)
    means = [jnp.mean(k_ref[n * blk:(n + 1) * blk, :].astype(F32), axis=0, keepdims=True)
             for n in range(nb)]
    km = jnp.concatenate(means, axis=0)
    km_hi = km.astype(BF16)
    km_lo = (km - km_hi.astype(F32)).astype(BF16)
    row = lax.broadcasted_iota(jnp.int32, (nb, blk), 0).astype(F32)
    bvec = jnp.broadcast_to(bvec_ref[...] * LOG2E, (blk, 2 * blk))
    toep = pltpu.roll(bvec, 0, 1, stride=1, stride_axis=0)
    krow = lax.broadcasted_iota(jnp.int32, (blk, blk), 0)
    qcol = lax.broadcasted_iota(jnp.int32, (blk, blk), 1)
    town = jnp.where(qcol >= krow, toep[:, 0:blk], NEG)
    tadj = toep[:, blk:2 * blk]

    def prep(i):
        q = q_ref[i * blk:(i + 1) * blk, :]
        gate = _dot_nt(km_hi, q) + _dot_nt(km_lo, q)
        valid = row < float(i)
        gm = jnp.where(valid, gate, -jnp.inf)
        sel = jnp.full(gate.shape, NEG, F32)
        for _ in range(min(MOBA_TOPK, nb)):
            mx = jnp.max(gm, axis=0, keepdims=True)
            first = jnp.min(jnp.where(gm == mx, row, float(nb)), axis=0, keepdims=True)
            pick = row == first
            sel = jnp.where(jnp.logical_and(pick, valid), 0.0, sel)
            gm = jnp.where(pick, -jnp.inf, gm)

        qs = (q.astype(F32) * (HEAD_DIM ** -0.5 * LOG2E)).astype(BF16)
        return qs, (sel, sel + b31 * LOG2E)

    def bias(i, j, aux):
        sel, far = aux
        if j == i:
            return town
        if j == i - 1:
            return tadj + sel[j:j + 1, :]
        return far[j:j + 1, :]

    _block_attention(nb, blk, MOBA_ISSUE, k_ref, vt_ref, s_ref, o_ref, prep, bias,
                     filler=retention_chunk, fill_every=RET_FILL_EVERY)


def _even_mixer(proj, log_g, cos2, sin2, gn_w, b31, bvec, batch, seq):
    assert RET_HEADS == MOBA_HEADS
    t = proj.shape[0]
    nh = MOBA_HEADS
    blk = MOBA_BLOCK

    def head_spec(group):
        return pl.BlockSpec((seq, HEAD_DIM), lambda b, h: (b, group * nh + h))

    out_spec = pl.BlockSpec((seq, HEAD_DIM), lambda b, h: (b, h))
    out_shape = jax.ShapeDtypeStruct((t, nh * HEAD_DIM), BF16)
    return pl.pallas_call(
        _even_mixer_kernel,
        grid=(batch, nh),
        in_specs=[pl.BlockSpec(memory_space=pltpu.SMEM), pl.BlockSpec(memory_space=pltpu.SMEM),
                  head_spec(0), head_spec(1), head_spec(2), head_spec(3),
                  _resident((seq, HEAD_DIM)), _resident((seq, HEAD_DIM)),
                  pl.BlockSpec((1, HEAD_DIM), lambda b, h: (0, h)),
                  head_spec(4), head_spec(5), head_spec(6),
                  pl.BlockSpec((None, 1, 2 * blk), lambda b, h: (h, 0, 0))],
        out_specs=[out_spec, out_spec],
        out_shape=[out_shape, out_shape],
        scratch_shapes=[pltpu.VMEM((HEAD_DIM + BF16_SUBLANES, seq), BF16),
                        pltpu.VMEM((2, seq, blk), F32)],
        compiler_params=_params("parallel", "parallel"),
        name="even_mixer",
    )(log_g, b31, proj, proj, proj, proj, cos2, sin2, gn_w, proj, proj, proj, bvec)


def _fox_gate_kernel(fl_ref, bf_ref, o_ref):
    x = fl_ref[...] + bf_ref[...]
    y = jnp.minimum(x, 0.0) - jnp.log1p(jnp.exp(-jnp.abs(x)))
    n = y.shape[1]
    lane = lax.broadcasted_iota(jnp.int32, y.shape, 1)
    sh = 1
    while sh < n:
        y = y + jnp.where(lane >= sh, pltpu.roll(y, sh, 1), 0.0)
        sh *= 2
    o_ref[...] = y


def _fox_gate(fl_t, b_forget, batch, seq):
    nh = fl_t.shape[0]
    return pl.pallas_call(
        _fox_gate_kernel,
        grid=(batch,),
        in_specs=[pl.BlockSpec((nh, seq), lambda b: (0, b)), _resident((nh, 1))],
        out_specs=pl.BlockSpec((nh, seq), lambda b: (0, b)),
        out_shape=jax.ShapeDtypeStruct((nh, batch * seq), F32),
        compiler_params=_params("parallel"),
        name="fox_gate",
    )(fl_t, b_forget)


def _fox_kernel(q_ref, k_ref, v_ref, f_ref, o_ref, ka_ref, vt_ref, s_ref):
    blk = FOX_BLOCK
    nb = k_ref.shape[0] // blk
    d = HEAD_DIM
    frow = f_ref[pl.ds(pl.program_id(1), 1), :] * (-LOG2E)
    hi = frow.astype(BF16).astype(F32)
    mid = (frow - hi).astype(BF16).astype(F32)
    lo = frow - hi - mid
    sub = lax.broadcasted_iota(jnp.int32, (8, frow.shape[1]), 0)
    top = jnp.where(sub == 0, hi, jnp.where(sub == 1, mid, jnp.where(sub == 2, lo, 0.0)))
    pad = jnp.zeros((d - 8, blk), F32)
    lane = lax.broadcasted_iota(jnp.int32, (blk, d), 1)
    for n in range(nb):
        rows = slice(n * blk, (n + 1) * blk)
        aug = jnp.transpose(jnp.concatenate([top[:, rows], pad], axis=0))
        ka_ref[rows, 0:d] = k_ref[rows, :]
        ka_ref[rows, d:2 * d] = aug.astype(BF16)
    _fill_values(v_ref, vt_ref, blk)
    ones = jnp.where(lane < 3, 1.0, 0.0).astype(BF16)
    krow = lax.broadcasted_iota(jnp.int32, (blk, blk), 0)
    qcol = lax.broadcasted_iota(jnp.int32, (blk, blk), 1)
    causal = jnp.where(qcol >= krow, 0.0, NEG)

    def prep(i):
        q = q_ref[i * blk:(i + 1) * blk, :].astype(F32)
        return jnp.concatenate([(q * (d ** -0.5 * LOG2E)).astype(BF16), ones], axis=1), None

    def bias(i, j, aux):
        return causal if j == i else None

    _block_attention(nb, blk, FOX_ISSUE, ka_ref, vt_ref, s_ref, o_ref, prep, bias)


def _fox(qkv, f, batch, seq):
    t = qkv.shape[0]
    nh = FOX_HEADS
    blk = FOX_BLOCK

    def head_spec(group):
        return pl.BlockSpec((seq, HEAD_DIM), lambda b, h: (b, group * nh + h))

    return pl.pallas_call(
        _fox_kernel,
        grid=(batch, nh),
        in_specs=[head_spec(0), head_spec(1), head_spec(2),
                  pl.BlockSpec((nh, seq), lambda b, h: (0, b))],
        out_specs=pl.BlockSpec((seq, HEAD_DIM), lambda b, h: (b, h)),
        out_shape=jax.ShapeDtypeStruct((t, nh * HEAD_DIM), BF16),
        scratch_shapes=[pltpu.VMEM((seq, 2 * HEAD_DIM), BF16),
                        pltpu.VMEM((HEAD_DIM + BF16_SUBLANES, seq), BF16),
                        pltpu.VMEM((2, seq, blk), F32)],
        compiler_params=_params("parallel", "parallel"),
        name="fox",
    )(qkv, qkv, qkv, f)


def _rotary_tables(seq):
    inv_freq = jnp.power(ROPE_BASE, -jnp.arange(0, HEAD_DIM, 2, dtype=F32) / HEAD_DIM)
    ang = jnp.arange(seq, dtype=F32)[:, None] * inv_freq[None, :]
    cos, sin = jnp.cos(ang), jnp.sin(ang)
    return jnp.concatenate([cos, cos], axis=-1), jnp.concatenate([-sin, sin], axis=-1)


def _t5_bucket(rel):
    n = jnp.maximum(rel, 0)
    max_exact = N_REL_BUCKETS // 2
    nf = jnp.maximum(n, 1).astype(F32)
    large = max_exact + (jnp.log(nf / max_exact) / math.log(REL_MAX_DISTANCE / max_exact)
                         * (N_REL_BUCKETS - max_exact)).astype(jnp.int32)
    large = jnp.minimum(large, N_REL_BUCKETS - 1)
    return jnp.where(n < max_exact, n, large)


def _moba_bias_vectors(rel_bias):
    blk = MOBA_BLOCK
    bias_t = rel_bias.astype(F32).T
    hot = _t5_bucket(jnp.arange(2 * blk))[:, None] == jnp.arange(N_REL_BUCKETS)
    near = jnp.sum(jnp.where(hot[None], bias_t[:, None, :], 0.0), axis=-1)
    far = bias_t[:, _t5_bucket(jnp.array(2 * blk, jnp.int32))]
    return far, near[:, None, :]


def kernel(x, rel_bias, norm_mix_pre, norm_mix_post, norm_ffn_pre, norm_ffn_post,
           even_w_in, even_gn, even_w_out, odd_w_in, odd_b_forget, odd_w_out,
           ffn_w_in, ffn_w_out):
    batch, seq, d = x.shape
    depth = norm_mix_pre.shape[0]
    xt = x.reshape(batch * seq, d)

    cos2, sin2 = _rotary_tables(seq)
    log_g = jnp.log1p(-jnp.power(2.0, -5.0 - jnp.arange(RET_HEADS, dtype=F32)))
    b31, bvec = _moba_bias_vectors(rel_bias)

    def row(v):
        return v.reshape(1, -1).astype(F32)

    for layer in range(depth):
        i = layer // 2
        if layer % 2 == 0:
            proj = _norm_proj(xt, row(norm_mix_pre[layer]), even_w_in[i].astype(BF16))
            ret, mob = _even_mixer(proj, log_g, cos2, sin2, row(even_gn[i]), b31, bvec,
                                   batch, seq)
            acts, w_mix = [ret, mob], even_w_out[i]
        else:
            w_in = jnp.pad(odd_w_in[i], ((0, 0), (0, HEAD_DIM - FOX_HEADS))).astype(BF16)
            qkv, fl_t = _norm_proj_gate(xt, row(norm_mix_pre[layer]), w_in, FOX_HEADS)
            f = _fox_gate(fl_t, odd_b_forget[i].reshape(-1, 1).astype(F32), batch, seq)
            acts, w_mix = [_fox(qkv, f, batch, seq)], odd_w_out[i]
        xt = _mix_ffn(acts, w_mix.astype(BF16), row(norm_mix_post[layer]), xt,
                      row(norm_ffn_pre[layer]), ffn_w_in[layer].astype(BF16),
                      ffn_w_out[layer].astype(BF16), row(norm_ffn_post[layer]))
    return xt.reshape(batch, seq, d)
```

```python
import functools
import math

import jax
import jax.numpy as jnp
import numpy as np
from jax import lax
from jax.experimental import pallas as pl
from jax.experimental.pallas import tpu as pltpu

F32 = jnp.float32
BF16 = jnp.bfloat16

HEAD_DIM = 128
RET_HEADS = 4
MOBA_HEADS = 4
FOX_HEADS = 8
RET_CHUNK = 128
MOBA_BLOCK = 256
MOBA_TOPK = 3
N_REL_BUCKETS = 32
REL_MAX_DISTANCE = 128
ROPE_BASE = 10000.0
RMS_EPS = 1e-6
GN_EPS = 1e-5
NEG = -1e30
LOG2E = math.log2(math.e)

V7X_VMEM_BYTES = 64 * 1024 * 1024
VMEM_LIMIT = V7X_VMEM_BYTES - 8 * 1024 * 1024

ROW_TILE = 1024
MIX_ROWS = 1024
SUB_ROWS = 512
COL_CHUNK = 512
FFN_CHUNK = 256
FOX_BLOCK = 256
BF16_SUBLANES = 16
MOBA_ISSUE = (2, 5)
RET_FILL_EVERY = 1
FOX_ISSUE = (4, 4)


def _params(*sem, flags=None):
    return pltpu.CompilerParams(dimension_semantics=sem, vmem_limit_bytes=VMEM_LIMIT,
                                flags=flags)


def _resident(shape):
    nd = len(shape)
    return pl.BlockSpec(shape, lambda *_: (0,) * nd, pipeline_mode=pl.Buffered(1))


def _rms(x, g):
    return x * lax.rsqrt(jnp.mean(x * x, axis=-1, keepdims=True) + RMS_EPS) * g


def _dot(a, b):
    return jnp.dot(a, b, preferred_element_type=F32)


def _dot_nt(a, b):
    return lax.dot_general(a, b, (((1,), (1,)), ((), ())), preferred_element_type=F32)


def _sub_tiles(rows, sub):
    return [slice(r, r + sub) for r in range(0, rows, sub)]


def _norm_proj_kernel(x_ref, g_ref, w_ref, o_ref):
    subs = _sub_tiles(x_ref.shape[0], SUB_ROWS)
    hns = [_rms(x_ref[rows, :], g_ref[...]).astype(BF16) for rows in subs]
    for rows, hn in zip(subs, hns):
        for c in range(o_ref.shape[1] // COL_CHUNK):
            sl = slice(c * COL_CHUNK, (c + 1) * COL_CHUNK)
            o_ref[rows, sl] = _dot(hn, w_ref[:, sl]).astype(o_ref.dtype)


def _norm_proj(x, g, w):
    t, d = x.shape
    n = w.shape[1]
    return pl.pallas_call(
        _norm_proj_kernel,
        grid=(t // ROW_TILE,),
        in_specs=[pl.BlockSpec((ROW_TILE, d), lambda i: (i, 0)),
                  _resident((1, d)), _resident((d, n))],
        out_specs=pl.BlockSpec((ROW_TILE, n), lambda i: (i, 0)),
        out_shape=jax.ShapeDtypeStruct((t, n), BF16),
        compiler_params=_params("parallel"),
        name="norm_proj",
    )(x, g, w)


def _norm_proj_gate_kernel(x_ref, g_ref, w_ref, wf_ref, o_ref, f_ref):
    n = o_ref.shape[1]
    subs = _sub_tiles(x_ref.shape[0], SUB_ROWS)
    hns = [_rms(x_ref[rows, :], g_ref[...]).astype(BF16) for rows in subs]
    for rows, hn in zip(subs, hns):
        for c in range(n // COL_CHUNK):
            sl = slice(c * COL_CHUNK, (c + 1) * COL_CHUNK)
            o_ref[rows, sl] = _dot(hn, w_ref[:, sl]).astype(o_ref.dtype)
        fl = _dot(hn, wf_ref[...])
        f_ref[:, rows] = jnp.transpose(fl)[0:f_ref.shape[0], :]


def _norm_proj_gate(x, g, w, wf, nh):
    t, d = x.shape
    n = w.shape[1]
    return pl.pallas_call(
        _norm_proj_gate_kernel,
        grid=(t // ROW_TILE,),
        in_specs=[pl.BlockSpec((ROW_TILE, d), lambda i: (i, 0)),
                  _resident((1, d)), _resident(w.shape), _resident(wf.shape)],
        out_specs=[pl.BlockSpec((ROW_TILE, n), lambda i: (i, 0)),
                   pl.BlockSpec((nh, ROW_TILE), lambda i: (0, i))],
        out_shape=[jax.ShapeDtypeStruct((t, n), BF16),
                   jax.ShapeDtypeStruct((nh, t), F32)],
        compiler_params=_params("parallel"),
        name="norm_proj_gate",
    )(x, g, w, wf)


def _mix_ffn_kernel(n_in, *refs):
    a_refs = refs[:n_in]
    wmix_ref, gmix_ref, x_ref, gpre_ref, wffn_ref, wo_ref, gpost_ref, o_ref = refs[n_in:]
    hidden = wo_ref.shape[0]
    subs = _sub_tiles(x_ref.shape[0], SUB_ROWS)

    xs, hns = [], []
    for rows in subs:
        m = None
        off = 0
        for a_ref in a_refs:
            width = a_ref.shape[1]
            part = _dot(a_ref[rows, :], wmix_ref[off:off + width, :])
            m = part if m is None else m + part
            off += width
        x = x_ref[rows, :] + _rms(m, gmix_ref[...])
        xs.append(x)
        hns.append(_rms(x, gpre_ref[...]).astype(BF16))
    for rows, x, hn in zip(subs, xs, hns):
        y = jnp.zeros(x.shape, F32)
        for c in range(hidden // FFN_CHUNK):
            lo = c * FFN_CHUNK
            gate = _dot(hn, wffn_ref[:, lo:lo + FFN_CHUNK])
            up = _dot(hn, wffn_ref[:, hidden + lo:hidden + lo + FFN_CHUNK])
            act = (gate * jax.nn.sigmoid(gate) * up).astype(BF16)
            y = y + _dot(act, wo_ref[lo:lo + FFN_CHUNK, :])
        o_ref[rows, :] = x + _rms(y, gpost_ref[...])


def _mix_ffn(acts, w_mix, g_mix, x, g_pre, w_ffn, w_o, g_post, layer):
    t, d = x.shape

    def layer_slab(w):
        return pl.BlockSpec((None,) + w.shape[1:], lambda i: (layer, 0, 0),
                            pipeline_mode=pl.Buffered(1))

    in_specs = [pl.BlockSpec((MIX_ROWS, a.shape[1]), lambda i: (i, 0)) for a in acts]
    in_specs += [_resident(w_mix.shape), _resident((1, d)),
                 pl.BlockSpec((MIX_ROWS, d), lambda i: (i, 0)),
                 _resident((1, d)), layer_slab(w_ffn), layer_slab(w_o),
                 _resident((1, d))]
    return pl.pallas_call(
        functools.partial(_mix_ffn_kernel, len(acts)),
        grid=(t // MIX_ROWS,),
        in_specs=in_specs,
        out_specs=pl.BlockSpec((MIX_ROWS, d), lambda i: (i, 0)),
        out_shape=jax.ShapeDtypeStruct((t, d), F32),
        compiler_params=_params("parallel"),
        name="mix_ffn",
    )(*acts, w_mix, g_mix, x, g_pre, w_ffn, w_o, g_post)


def _retention_work(lg, q_ref, k_ref, v_ref, g_ref, cos_ref, sin_ref, gn_ref, o_ref):
    c = RET_CHUNK
    seq = q_ref.shape[0]
    row = lax.broadcasted_iota(jnp.int32, (c, c), 0).astype(F32)
    col = lax.broadcasted_iota(jnp.int32, (c, c), 1).astype(F32)
    diff = row - col
    inner = jnp.where(diff >= 0, jnp.exp(lg * jnp.maximum(diff, 0.0)), 0.0)
    idx = lax.broadcasted_iota(jnp.int32, (c, HEAD_DIM), 0).astype(F32)
    q_dec = jnp.exp(lg * (idx + 1.0))
    k_dec = jnp.exp(lg * (c - 1.0 - idx))
    chunk_dec = jnp.exp(lg * jnp.full((1, HEAD_DIM), float(c), F32))
    gn_w = gn_ref[...]
    scale = HEAD_DIM ** -0.5

    def rot(t, cos, sin):
        return t * cos + pltpu.roll(t, HEAD_DIM // 2, 1) * sin

    def head(n):
        sl = slice(n * c, (n + 1) * c)
        cos = cos_ref[sl, :]
        sin = sin_ref[sl, :]
        q = rot(q_ref[sl, :].astype(F32), cos, sin) * scale
        k = rot(k_ref[sl, :].astype(F32), cos, sin)
        qb = q.astype(BF16)
        return qb, k, _dot_nt(qb, k.astype(BF16))

    def tail(n, state, qb, k, sc):
        sl = slice(n * c, (n + 1) * c)
        v = v_ref[sl, :]
        kd_t = jnp.transpose(k * k_dec).astype(BF16)
        kv = _dot(kd_t, v)
        cross = _dot(qb, state.astype(BF16))
        o = _dot((sc * inner).astype(BF16), v) + cross * q_dec
        mu = jnp.mean(o, axis=-1, keepdims=True)
        var = jnp.mean(jnp.square(o - mu), axis=-1, keepdims=True)
        y = (o - mu) * lax.rsqrt(var + GN_EPS) * gn_w
        gate = g_ref[sl, :].astype(F32)
        o_ref[sl, :] = (y * (gate * jax.nn.sigmoid(gate))).astype(o_ref.dtype)
        return state * chunk_dec + kv

    n_chunks = seq // c
    work = {"n": 0, "state": jnp.zeros((HEAD_DIM, HEAD_DIM), F32), "ahead": head(0)}

    def emit():
        n = work["n"]
        if n >= n_chunks:
            return False
        cur = work["ahead"]
        if n + 1 < n_chunks:
            work["ahead"] = head(n + 1)
        work["state"] = tail(n, work["state"], *cur)
        work["n"] = n + 1
        return n + 1 < n_chunks

    return emit


def _group_reduce(x, op):
    r, c = x.shape
    g = r // 8
    if g % 4 == 0:
        return op(op(x.reshape(4, g // 4, 8, c), axis=1), axis=0)
    return op(x.reshape(g, 8, c), axis=0)


def _block_attention(nb, blk, issue, keys_ref, vt_ref, s_ref, o_ref, prep, bias,
                     filler=None, fill_every=1):
    d = HEAD_DIM
    chunk, lag = issue
    steps = 0

    def chunks(i):
        return [(lo, min(lo + chunk, i + 1)) for lo in range(0, i + 1, chunk)]

    def score_chunk(i, lo, hi, qa, aux, m8, snaps):
        s = _dot_nt(keys_ref[lo * blk:hi * blk, :], qa)
        for j in range(lo, hi):
            t = s[(j - lo) * blk:(j - lo + 1) * blk, :]
            b = bias(i, j, aux)
            if b is not None:
                t = t + b
            s_ref[i % 2, j * blk:(j + 1) * blk, :] = t
            t8 = _group_reduce(t, jnp.max)
            m8 = t8 if m8 is None else jnp.maximum(m8, t8)
            snaps[j] = m8
        return m8

    def attend_chunk(i, lo, hi, m, acc, snaps):
        ps = []
        for j in range(lo, hi):
            snap = snaps.get(j - lag)
            m_j = m if snap is None else jnp.minimum(m, jnp.maximum(m, snap[0:1, :]))
            ps.append(jnp.exp2(s_ref[i % 2, j * blk:(j + 1) * blk, :] - m_j).astype(BF16))
        p = ps[0] if len(ps) == 1 else jnp.concatenate(ps, axis=0)
        part = _dot(vt_ref[:, lo * blk:hi * blk], p)
        return part if acc is None else acc + part

    qa, aux = prep(nb - 1)
    m8 = None
    for lo, hi in chunks(nb - 1):
        m8 = score_chunk(nb - 1, lo, hi, qa, aux, m8, {})
    for i in reversed(range(nb)):
        m = jnp.max(m8, axis=0, keepdims=True)
        if i > 0:
            qa, aux = prep(i - 1)
        ahead = chunks(i - 1) if i > 0 else []
        here = chunks(i)
        m8 = None
        acc = None
        snaps = {}
        for c in range(len(here)):
            done = dict(snaps)
            if c < len(ahead):
                m8 = score_chunk(i - 1, *ahead[c], qa, aux, m8, snaps)
            acc = attend_chunk(i, *here[c], m, acc, done)
            steps += 1
            if filler is not None and steps % fill_every == 0:
                filler()
        out = acc[0:d, :] / acc[d:d + 1, :]
        o_ref[i * blk:(i + 1) * blk, :] = jnp.transpose(out).astype(o_ref.dtype)
    while filler is not None and filler():
        pass


def _fill_values(v_ref, vt_ref, blk):
    d = HEAD_DIM
    seq = v_ref.shape[0]
    for n in range(seq // blk):
        rows = slice(n * blk, (n + 1) * blk)
        vt_ref[0:d, rows] = jnp.transpose(v_ref[rows, :].astype(F32)).astype(BF16)
    first = lax.broadcasted_iota(jnp.int32, (vt_ref.shape[0] - d, seq), 0) == 0
    vt_ref[d:, :] = jnp.where(first, 1.0, 0.0).astype(BF16)


def _even_mixer_kernel(lg_ref, b31_ref, rq_ref, rk_ref, rv_ref, rg_ref, cos_ref, sin_ref,
                       gn_ref, q_ref, k_ref, v_ref, bvec_ref, ret_ref, o_ref, vt_ref, s_ref):
    blk = MOBA_BLOCK
    nb = k_ref.shape[0] // blk
    h = pl.program_id(1)
    b31 = b31_ref[h]
    retention_chunk = _retention_work(lg_ref[h], rq_ref, rk_ref, rv_ref, rg_ref,
                                      cos_ref, sin_ref, gn_ref, ret_ref)

    _fill_values(v_ref, vt_ref, blk)
    means = [jnp.mean(k_ref[n * blk:(n + 1) * blk, :].astype(F32), axis=0, keepdims=True)
             for n in range(nb)]
    km = jnp.concatenate(means, axis=0)
    km_hi = km.astype(BF16)
    km_lo = (km - km_hi.astype(F32)).astype(BF16)
    row = lax.broadcasted_iota(jnp.int32, (nb, blk), 0).astype(F32)
    bvec = jnp.broadcast_to(bvec_ref[...] * LOG2E, (blk, 2 * blk))
    toep = pltpu.roll(bvec, 0, 1, stride=1, stride_axis=0)
    krow = lax.broadcasted_iota(jnp.int32, (blk, blk), 0)
    qcol = lax.broadcasted_iota(jnp.int32, (blk, blk), 1)
    town = jnp.where(qcol >= krow, toep[:, 0:blk], NEG)
    tadj = toep[:, blk:2 * blk]

    def prep(i):
        q = q_ref[i * blk:(i + 1) * blk, :]
        gate = _dot_nt(km_hi, q) + _dot_nt(km_lo, q)
        valid = row < float(i)
        gm = jnp.where(valid, gate, -jnp.inf)
        sel = jnp.full(gate.shape, NEG, F32)
        for _ in range(min(MOBA_TOPK, nb)):
            mx = jnp.max(gm, axis=0, keepdims=True)
            first = jnp.min(jnp.where(gm == mx, row, float(nb)), axis=0, keepdims=True)
            pick = row == first
            sel = jnp.where(jnp.logical_and(pick, valid), 0.0, sel)
            gm = jnp.where(pick, -jnp.inf, gm)

        qs = (q.astype(F32) * (HEAD_DIM ** -0.5 * LOG2E)).astype(BF16)
        return qs, (sel, sel + b31 * LOG2E)

    def bias(i, j, aux):
        sel, far = aux
        if j == i:
            return town
        if j == i - 1:
            return tadj + sel[j:j + 1, :]
        return far[j:j + 1, :]

    _block_attention(nb, blk, MOBA_ISSUE, k_ref, vt_ref, s_ref, o_ref, prep, bias,
                     filler=retention_chunk, fill_every=RET_FILL_EVERY)


def _even_mixer(proj, log_g, cos2, sin2, gn_w, b31, bvec, batch, seq):
    assert RET_HEADS == MOBA_HEADS
    t = proj.shape[0]
    nh = MOBA_HEADS
    blk = MOBA_BLOCK

    def head_spec(group):
        return pl.BlockSpec((seq, HEAD_DIM), lambda b, h: (b, group * nh + h))

    out_spec = pl.BlockSpec((seq, HEAD_DIM), lambda b, h: (b, h))
    out_shape = jax.ShapeDtypeStruct((t, nh * HEAD_DIM), BF16)
    return pl.pallas_call(
        _even_mixer_kernel,
        grid=(batch, nh),
        in_specs=[pl.BlockSpec(memory_space=pltpu.SMEM), pl.BlockSpec(memory_space=pltpu.SMEM),
                  head_spec(0), head_spec(1), head_spec(2), head_spec(3),
                  _resident((seq, HEAD_DIM)), _resident((seq, HEAD_DIM)),
                  pl.BlockSpec((1, HEAD_DIM), lambda b, h: (0, h)),
                  head_spec(4), head_spec(5), head_spec(6),
                  pl.BlockSpec((None, 1, 2 * blk), lambda b, h: (h, 0, 0))],
        out_specs=[out_spec, out_spec],
        out_shape=[out_shape, out_shape],
        scratch_shapes=[pltpu.VMEM((HEAD_DIM + BF16_SUBLANES, seq), BF16),
                        pltpu.VMEM((2, seq, blk), F32)],
        compiler_params=_params("parallel", "parallel"),
        name="even_mixer",
    )(log_g, b31, proj, proj, proj, proj, cos2, sin2, gn_w, proj, proj, proj, bvec)


def _fox_gate_kernel(fl_ref, bf_ref, o_ref):
    x = fl_ref[...] + bf_ref[...]
    y = jnp.minimum(x, 0.0) - jnp.log1p(jnp.exp(-jnp.abs(x)))
    n = y.shape[1]
    lane = lax.broadcasted_iota(jnp.int32, y.shape, 1)
    sh = 1
    while sh < n:
        y = y + jnp.where(lane >= sh, pltpu.roll(y, sh, 1), 0.0)
        sh *= 2
    o_ref[...] = y


def _fox_gate(fl_t, b_forget, batch, seq):
    nh = fl_t.shape[0]
    return pl.pallas_call(
        _fox_gate_kernel,
        grid=(batch,),
        in_specs=[pl.BlockSpec((nh, seq), lambda b: (0, b)), _resident((nh, 1))],
        out_specs=pl.BlockSpec((nh, seq), lambda b: (0, b)),
        out_shape=jax.ShapeDtypeStruct((nh, batch * seq), F32),
        compiler_params=_params("parallel"),
        name="fox_gate",
    )(fl_t, b_forget)


def _fox_kernel(q_ref, k_ref, v_ref, f_ref, o_ref, ka_ref, vt_ref, s_ref):
    blk = FOX_BLOCK
    nb = k_ref.shape[0] // blk
    d = HEAD_DIM
    frow = f_ref[pl.ds(pl.program_id(1), 1), :] * (-LOG2E)
    hi = frow.astype(BF16).astype(F32)
    mid = (frow - hi).astype(BF16).astype(F32)
    lo = frow - hi - mid
    sub = lax.broadcasted_iota(jnp.int32, (8, frow.shape[1]), 0)
    top = jnp.where(sub == 0, hi, jnp.where(sub == 1, mid, jnp.where(sub == 2, lo, 0.0)))
    pad = jnp.zeros((d - 8, blk), F32)
    lane = lax.broadcasted_iota(jnp.int32, (blk, d), 1)
    for n in range(nb):
        rows = slice(n * blk, (n + 1) * blk)
        aug = jnp.transpose(jnp.concatenate([top[:, rows], pad], axis=0))
        ka_ref[rows, 0:d] = k_ref[rows, :]
        ka_ref[rows, d:2 * d] = aug.astype(BF16)
    _fill_values(v_ref, vt_ref, blk)
    ones = jnp.where(lane < 3, 1.0, 0.0).astype(BF16)
    krow = lax.broadcasted_iota(jnp.int32, (blk, blk), 0)
    qcol = lax.broadcasted_iota(jnp.int32, (blk, blk), 1)
    causal = jnp.where(qcol >= krow, 0.0, NEG)

    def prep(i):
        q = q_ref[i * blk:(i + 1) * blk, :].astype(F32)
        return jnp.concatenate([(q * (d ** -0.5 * LOG2E)).astype(BF16), ones], axis=1), None

    def bias(i, j, aux):
        return causal if j == i else None

    _block_attention(nb, blk, FOX_ISSUE, ka_ref, vt_ref, s_ref, o_ref, prep, bias)


def _fox(qkv, f, batch, seq):
    t = qkv.shape[0]
    nh = FOX_HEADS
    blk = FOX_BLOCK

    def head_spec(group):
        return pl.BlockSpec((seq, HEAD_DIM), lambda b, h: (b, group * nh + h))

    return pl.pallas_call(
        _fox_kernel,
        grid=(batch, nh),
        in_specs=[head_spec(0), head_spec(1), head_spec(2),
                  pl.BlockSpec((nh, seq), lambda b, h: (0, b))],
        out_specs=pl.BlockSpec((seq, HEAD_DIM), lambda b, h: (b, h)),
        out_shape=jax.ShapeDtypeStruct((t, nh * HEAD_DIM), BF16),
        scratch_shapes=[pltpu.VMEM((seq, 2 * HEAD_DIM), BF16),
                        pltpu.VMEM((HEAD_DIM + BF16_SUBLANES, seq), BF16),
                        pltpu.VMEM((2, seq, blk), F32)],
        compiler_params=_params("parallel", "parallel"),
        name="fox",
    )(qkv, qkv, qkv, f)


def _rotary_tables(seq):
    inv_freq = np.power(ROPE_BASE, -np.arange(0, HEAD_DIM, 2, dtype=np.float64) / HEAD_DIM)
    ang = np.arange(seq, dtype=np.float64)[:, None] * inv_freq[None, :]
    cos, sin = np.cos(ang), np.sin(ang)
    return (jnp.asarray(np.concatenate([cos, cos], axis=-1), F32),
            jnp.asarray(np.concatenate([-sin, sin], axis=-1), F32))


def _t5_bucket(rel):
    n = jnp.maximum(rel, 0)
    max_exact = N_REL_BUCKETS // 2
    nf = jnp.maximum(n, 1).astype(F32)
    large = max_exact + (jnp.log(nf / max_exact) / math.log(REL_MAX_DISTANCE / max_exact)
                         * (N_REL_BUCKETS - max_exact)).astype(jnp.int32)
    large = jnp.minimum(large, N_REL_BUCKETS - 1)
    return jnp.where(n < max_exact, n, large)


def _moba_bias_vectors(rel_bias):
    blk = MOBA_BLOCK
    bias_t = rel_bias.astype(F32).T
    hot = _t5_bucket(jnp.arange(2 * blk))[:, None] == jnp.arange(N_REL_BUCKETS)
    near = jnp.sum(jnp.where(hot[None], bias_t[:, None, :], 0.0), axis=-1)
    far = bias_t[:, _t5_bucket(jnp.array(2 * blk, jnp.int32))]
    return far, near[:, None, :]


def kernel(x, rel_bias, norm_mix_pre, norm_mix_post, norm_ffn_pre, norm_ffn_post,
           even_w_in, even_gn, even_w_out, odd_w_in, odd_b_forget, odd_w_out,
           ffn_w_in, ffn_w_out):
    batch, seq, d = x.shape
    depth = norm_mix_pre.shape[0]
    xt = x.reshape(batch * seq, d)

    cos2, sin2 = _rotary_tables(seq)
    log_g = jnp.log1p(-jnp.power(2.0, -5.0 - jnp.arange(RET_HEADS, dtype=F32)))
    b31, bvec = _moba_bias_vectors(rel_bias)

    def row(v):
        return v.reshape(1, -1).astype(F32)

    w_ffn_all = ffn_w_in.astype(BF16)
    w_o_all = ffn_w_out.astype(BF16)
    for layer in range(depth):
        i = layer // 2
        if layer % 2 == 0:
            proj = _norm_proj(xt, row(norm_mix_pre[layer]), even_w_in[i].astype(BF16))
            ret, mob = _even_mixer(proj, log_g, cos2, sin2, row(even_gn[i]), b31, bvec,
                                   batch, seq)
            acts, w_mix = [ret, mob], even_w_out[i]
        else:
            n_qkv = 3 * FOX_HEADS * HEAD_DIM
            w_gate = jnp.pad(odd_w_in[i][:, n_qkv:], ((0, 0), (0, HEAD_DIM - FOX_HEADS)))
            qkv, fl_t = _norm_proj_gate(xt, row(norm_mix_pre[layer]),
                                        odd_w_in[i][:, :n_qkv].astype(BF16),
                                        w_gate.astype(BF16), FOX_HEADS)
            f = _fox_gate(fl_t, odd_b_forget[i].reshape(-1, 1).astype(F32), batch, seq)
            acts, w_mix = [_fox(qkv, f, batch, seq)], odd_w_out[i]
        xt = _mix_ffn(acts, w_mix.astype(BF16), row(norm_mix_post[layer]), xt,
                      row(norm_ffn_pre[layer]), w_ffn_all, w_o_all,
                      row(norm_ffn_post[layer]), layer)
    return xt.reshape(batch, seq, d)
```

```python
import functools
import math
from typing import Any, Callable, NamedTuple

import jax
import jax.numpy as jnp
import numpy as np
from jax import lax
from jax.experimental import pallas as pl
from jax.experimental.pallas import tpu as pltpu

F32 = jnp.float32
BF16 = jnp.bfloat16

HEAD_DIM = 128
RET_HEADS = 4
MOBA_HEADS = 4
FOX_HEADS = 8
RET_CHUNK = 128
MOBA_BLOCK = 256
MOBA_TOPK = 3
N_REL_BUCKETS = 32
REL_MAX_DISTANCE = 128
ROPE_BASE = 10000.0
RMS_EPS = 1e-6
GN_EPS = 1e-5
NEG = -1e30
LOG2E = math.log2(math.e)

V7X_VMEM_BYTES = 64 * 1024 * 1024
VMEM_LIMIT = V7X_VMEM_BYTES - 8 * 1024 * 1024

ROW_TILE = 1024
MIX_ROWS = 1024
SUB_ROWS = 512
COL_CHUNK = 512
FFN_CHUNK = 256
FOX_BLOCK = 256
FOX_GROUP = 2
BF16_SUBLANES = 16
MOBA_ISSUE = (2, 5)
RET_FILL_EVERY = 1
FOX_ISSUE = (4, 4)


def _params(*sem, flags=None):
    return pltpu.CompilerParams(dimension_semantics=sem, vmem_limit_bytes=VMEM_LIMIT,
                                flags=flags)


def _resident(shape):
    nd = len(shape)
    return pl.BlockSpec(shape, lambda *_: (0,) * nd, pipeline_mode=pl.Buffered(1))


def _rms(x, g):
    return x * lax.rsqrt(jnp.mean(x * x, axis=-1, keepdims=True) + RMS_EPS) * g


def _dot(a, b):
    return jnp.dot(a, b, preferred_element_type=F32)


def _dot_nt(a, b):
    return lax.dot_general(a, b, (((1,), (1,)), ((), ())), preferred_element_type=F32)


def _sub_tiles(rows, sub):
    return [slice(r, r + sub) for r in range(0, rows, sub)]


def _norm_proj_kernel(x_ref, g_ref, w_ref, o_ref):
    subs = _sub_tiles(x_ref.shape[0], SUB_ROWS)
    hns = [_rms(x_ref[rows, :], g_ref[...]).astype(BF16) for rows in subs]
    for rows, hn in zip(subs, hns):
        for c in range(o_ref.shape[1] // COL_CHUNK):
            sl = slice(c * COL_CHUNK, (c + 1) * COL_CHUNK)
            o_ref[rows, sl] = _dot(hn, w_ref[:, sl]).astype(o_ref.dtype)


def _norm_proj(x, g, w):
    t, d = x.shape
    n = w.shape[1]
    return pl.pallas_call(
        _norm_proj_kernel,
        grid=(t // ROW_TILE,),
        in_specs=[pl.BlockSpec((ROW_TILE, d), lambda i: (i, 0)),
                  _resident((1, d)), _resident((d, n))],
        out_specs=pl.BlockSpec((ROW_TILE, n), lambda i: (i, 0)),
        out_shape=jax.ShapeDtypeStruct((t, n), BF16),
        compiler_params=_params("parallel"),
        name="norm_proj",
    )(x, g, w)


def _norm_proj_gate_kernel(x_ref, g_ref, w_ref, wf_ref, o_ref, f_ref):
    n = o_ref.shape[1]
    subs = _sub_tiles(x_ref.shape[0], SUB_ROWS)
    hns = [_rms(x_ref[rows, :], g_ref[...]).astype(BF16) for rows in subs]
    for rows, hn in zip(subs, hns):
        for c in range(n // COL_CHUNK):
            sl = slice(c * COL_CHUNK, (c + 1) * COL_CHUNK)
            o_ref[rows, sl] = _dot(hn, w_ref[:, sl]).astype(o_ref.dtype)
        fl = _dot(hn, wf_ref[...])
        f_ref[:, rows] = jnp.transpose(fl)[0:f_ref.shape[0], :]


def _norm_proj_gate(x, g, w, wf, nh):
    t, d = x.shape
    n = w.shape[1]
    return pl.pallas_call(
        _norm_proj_gate_kernel,
        grid=(t // ROW_TILE,),
        in_specs=[pl.BlockSpec((ROW_TILE, d), lambda i: (i, 0)),
                  _resident((1, d)), _resident(w.shape), _resident(wf.shape)],
        out_specs=[pl.BlockSpec((ROW_TILE, n), lambda i: (i, 0)),
                   pl.BlockSpec((nh, ROW_TILE), lambda i: (0, i))],
        out_shape=[jax.ShapeDtypeStruct((t, n), BF16),
                   jax.ShapeDtypeStruct((nh, t), F32)],
        compiler_params=_params("parallel"),
        name="norm_proj_gate",
    )(x, g, w, wf)


def _mix_ffn_kernel(n_in, *refs):
    a_refs = refs[:n_in]
    wmix_ref, gmix_ref, x_ref, gpre_ref, wffn_ref, wo_ref, gpost_ref, o_ref = refs[n_in:]
    hidden = wo_ref.shape[0]
    subs = _sub_tiles(x_ref.shape[0], SUB_ROWS)

    xs, hns = [], []
    for rows in subs:
        m = None
        off = 0
        for a_ref in a_refs:
            width = a_ref.shape[1]
            part = _dot(a_ref[rows, :], wmix_ref[off:off + width, :])
            m = part if m is None else m + part
            off += width
        x = x_ref[rows, :] + _rms(m, gmix_ref[...])
        xs.append(x)
        hns.append(_rms(x, gpre_ref[...]).astype(BF16))
    for rows, x, hn in zip(subs, xs, hns):
        y = jnp.zeros(x.shape, F32)
        for c in range(hidden // FFN_CHUNK):
            lo = c * FFN_CHUNK
            gate = _dot(hn, wffn_ref[:, lo:lo + FFN_CHUNK])
            up = _dot(hn, wffn_ref[:, hidden + lo:hidden + lo + FFN_CHUNK])
            act = (gate * jax.nn.sigmoid(gate) * up).astype(BF16)
            y = y + _dot(act, wo_ref[lo:lo + FFN_CHUNK, :])
        o_ref[rows, :] = x + _rms(y, gpost_ref[...])


def _mix_ffn(acts, w_mix, g_mix, x, g_pre, w_ffn, w_o, g_post, layer):
    t, d = x.shape

    def layer_slab(w):
        return pl.BlockSpec((None,) + w.shape[1:], lambda i: (layer, 0, 0),
                            pipeline_mode=pl.Buffered(1))

    in_specs = [pl.BlockSpec((MIX_ROWS, a.shape[1]), lambda i: (i, 0)) for a in acts]
    in_specs += [_resident(w_mix.shape), _resident((1, d)),
                 pl.BlockSpec((MIX_ROWS, d), lambda i: (i, 0)),
                 _resident((1, d)), layer_slab(w_ffn), layer_slab(w_o),
                 _resident((1, d))]
    return pl.pallas_call(
        functools.partial(_mix_ffn_kernel, len(acts)),
        grid=(t // MIX_ROWS,),
        in_specs=in_specs,
        out_specs=pl.BlockSpec((MIX_ROWS, d), lambda i: (i, 0)),
        out_shape=jax.ShapeDtypeStruct((t, d), F32),
        compiler_params=_params("parallel"),
        name="mix_ffn",
    )(*acts, w_mix, g_mix, x, g_pre, w_ffn, w_o, g_post)


def _retention_work(lg, q_ref, k_ref, v_ref, g_ref, cos_ref, sin_ref, gn_ref, o_ref):
    c = RET_CHUNK
    seq = q_ref.shape[0]
    row = lax.broadcasted_iota(jnp.int32, (c, c), 0).astype(F32)
    col = lax.broadcasted_iota(jnp.int32, (c, c), 1).astype(F32)
    diff = row - col
    inner = jnp.where(diff >= 0, jnp.exp(lg * jnp.maximum(diff, 0.0)), 0.0)
    idx = lax.broadcasted_iota(jnp.int32, (c, HEAD_DIM), 0).astype(F32)
    q_dec = jnp.exp(lg * (idx + 1.0))
    k_dec = jnp.exp(lg * (c - 1.0 - idx))
    chunk_dec = jnp.exp(lg * jnp.full((1, HEAD_DIM), float(c), F32))
    gn_w = gn_ref[...]
    scale = HEAD_DIM ** -0.5

    def rot(t, cos, sin):
        return t * cos + pltpu.roll(t, HEAD_DIM // 2, 1) * sin

    def head(n):
        sl = slice(n * c, (n + 1) * c)
        cos = cos_ref[sl, :]
        sin = sin_ref[sl, :]
        q = rot(q_ref[sl, :].astype(F32), cos, sin) * scale
        k = rot(k_ref[sl, :].astype(F32), cos, sin)
        qb = q.astype(BF16)
        return qb, k, _dot_nt(qb, k.astype(BF16))

    def tail(n, state, qb, k, sc):
        sl = slice(n * c, (n + 1) * c)
        v = v_ref[sl, :]
        kd_t = jnp.transpose(k * k_dec).astype(BF16)
        kv = _dot(kd_t, v)
        cross = _dot(qb, state.astype(BF16))
        o = _dot((sc * inner).astype(BF16), v) + cross * q_dec
        mu = jnp.mean(o, axis=-1, keepdims=True)
        var = jnp.mean(jnp.square(o - mu), axis=-1, keepdims=True)
        y = (o - mu) * lax.rsqrt(var + GN_EPS) * gn_w
        gate = g_ref[sl, :].astype(F32)
        o_ref[sl, :] = (y * (gate * jax.nn.sigmoid(gate))).astype(o_ref.dtype)
        return state * chunk_dec + kv

    n_chunks = seq // c
    work = {"n": 0, "state": jnp.zeros((HEAD_DIM, HEAD_DIM), F32), "ahead": head(0)}

    def emit():
        n = work["n"]
        if n >= n_chunks:
            return False
        cur = work["ahead"]
        if n + 1 < n_chunks:
            work["ahead"] = head(n + 1)
        work["state"] = tail(n, work["state"], *cur)
        work["n"] = n + 1
        return n + 1 < n_chunks

    return emit


def _group_reduce(x, op):
    r, c = x.shape
    g = r // 8
    if g % 4 == 0:
        return op(op(x.reshape(4, g // 4, 8, c), axis=1), axis=0)
    return op(x.reshape(g, 8, c), axis=0)


class _AttnProblem(NamedTuple):
    keys: Any
    vt: Any
    scores: Any
    out: Any
    prep: Callable
    bias: Callable


def _block_attention(nb, blk, issue, problems, filler=None, fill_every=1):
    d = HEAD_DIM
    chunk, lag = issue
    steps = 0

    def chunks(i):
        return [(lo, min(lo + chunk, i + 1)) for lo in range(0, i + 1, chunk)]

    def score_chunk(p, i, lo, hi, qa, aux, m8, snaps):
        s = _dot_nt(p.keys[lo * blk:hi * blk, :], qa)
        for j in range(lo, hi):
            t = s[(j - lo) * blk:(j - lo + 1) * blk, :]
            b = p.bias(i, j, aux)
            if b is not None:
                t = t + b
            p.scores[i % 2, j * blk:(j + 1) * blk, :] = t
            t8 = _group_reduce(t, jnp.max)
            m8 = t8 if m8 is None else jnp.maximum(m8, t8)
            snaps[j] = m8
        return m8

    def attend_chunk(p, i, lo, hi, m, acc, snaps):
        ps = []
        for j in range(lo, hi):
            snap = snaps.get(j - lag)
            m_j = m if snap is None else jnp.minimum(m, jnp.maximum(m, snap[0:1, :]))
            ps.append(jnp.exp2(p.scores[i % 2, j * blk:(j + 1) * blk, :] - m_j).astype(BF16))
        pm = ps[0] if len(ps) == 1 else jnp.concatenate(ps, axis=0)
        part = _dot(p.vt[:, lo * blk:hi * blk], pm)
        return part if acc is None else acc + part

    states = []
    for p in problems:
        qa, aux = p.prep(nb - 1)
        m8 = None
        for lo, hi in chunks(nb - 1):
            m8 = score_chunk(p, nb - 1, lo, hi, qa, aux, m8, {})
        states.append({"m8": m8})
    for i in reversed(range(nb)):
        ahead = chunks(i - 1) if i > 0 else []
        here = chunks(i)
        for p, st in zip(problems, states):
            st["m"] = jnp.max(st["m8"], axis=0, keepdims=True)
            st["ops"] = p.prep(i - 1) if i > 0 else None
            st["m8"], st["acc"], st["snaps"] = None, None, {}
        for c in range(len(here)):
            for p, st in zip(problems, states):
                done = dict(st["snaps"])
                if c < len(ahead):
                    st["m8"] = score_chunk(p, i - 1, *ahead[c], *st["ops"], st["m8"],
                                           st["snaps"])
                st["acc"] = attend_chunk(p, i, *here[c], st["m"], st["acc"], done)
            steps += 1
            if filler is not None and steps % fill_every == 0:
                filler()
        for p, st in zip(problems, states):
            acc = st["acc"]
            out = acc[0:d, :] / acc[d:d + 1, :]
            p.out[i * blk:(i + 1) * blk, :] = jnp.transpose(out).astype(p.out.dtype)
    while filler is not None and filler():
        pass


def _fill_values(v_ref, vt_ref, blk):
    d = HEAD_DIM
    seq = v_ref.shape[0]
    for n in range(seq // blk):
        rows = slice(n * blk, (n + 1) * blk)
        vt_ref[0:d, rows] = jnp.transpose(v_ref[rows, :].astype(F32)).astype(BF16)
    first = lax.broadcasted_iota(jnp.int32, (vt_ref.shape[0] - d, seq), 0) == 0
    vt_ref[d:, :] = jnp.where(first, 1.0, 0.0).astype(BF16)


def _even_mixer_kernel(lg_ref, b31_ref, rq_ref, rk_ref, rv_ref, rg_ref, cos_ref, sin_ref,
                       gn_ref, q_ref, k_ref, v_ref, bvec_ref, ret_ref, o_ref, vt_ref, s_ref):
    blk = MOBA_BLOCK
    nb = k_ref.shape[0] // blk
    h = pl.program_id(1)
    b31 = b31_ref[h]
    retention_chunk = _retention_work(lg_ref[h], rq_ref, rk_ref, rv_ref, rg_ref,
                                      cos_ref, sin_ref, gn_ref, ret_ref)

    _fill_values(v_ref, vt_ref, blk)
    means = [jnp.mean(k_ref[n * blk:(n + 1) * blk, :].astype(F32), axis=0, keepdims=True)
             for n in range(nb)]
    km = jnp.concatenate(means, axis=0)
    km_hi = km.astype(BF16)
    km_lo = (km - km_hi.astype(F32)).astype(BF16)
    row = lax.broadcasted_iota(jnp.int32, (nb, blk), 0).astype(F32)
    bvec = jnp.broadcast_to(bvec_ref[...] * LOG2E, (blk, 2 * blk))
    toep = pltpu.roll(bvec, 0, 1, stride=1, stride_axis=0)
    krow = lax.broadcasted_iota(jnp.int32, (blk, blk), 0)
    qcol = lax.broadcasted_iota(jnp.int32, (blk, blk), 1)
    town = jnp.where(qcol >= krow, toep[:, 0:blk], NEG)
    tadj = toep[:, blk:2 * blk]

    def prep(i):
        q = q_ref[i * blk:(i + 1) * blk, :]
        gate = _dot_nt(km_hi, q) + _dot_nt(km_lo, q)
        valid = row < float(i)
        gm = jnp.where(valid, gate, -jnp.inf)
        sel = jnp.full(gate.shape, NEG, F32)
        for _ in range(min(MOBA_TOPK, nb)):
            mx = jnp.max(gm, axis=0, keepdims=True)
            first = jnp.min(jnp.where(gm == mx, row, float(nb)), axis=0, keepdims=True)
            pick = row == first
            sel = jnp.where(jnp.logical_and(pick, valid), 0.0, sel)
            gm = jnp.where(pick, -jnp.inf, gm)

        qs = (q.astype(F32) * (HEAD_DIM ** -0.5 * LOG2E)).astype(BF16)
        return qs, (sel, sel + b31 * LOG2E)

    def bias(i, j, aux):
        sel, far = aux
        if j == i:
            return town
        if j == i - 1:
            return tadj + sel[j:j + 1, :]
        return far[j:j + 1, :]

    _block_attention(nb, blk, MOBA_ISSUE,
                     [_AttnProblem(k_ref, vt_ref, s_ref, o_ref, prep, bias)],
                     filler=retention_chunk, fill_every=RET_FILL_EVERY)


def _even_mixer(proj, log_g, cos2, sin2, gn_w, b31, bvec, batch, seq):
    assert RET_HEADS == MOBA_HEADS
    t = proj.shape[0]
    nh = MOBA_HEADS
    blk = MOBA_BLOCK

    def head_spec(group):
        return pl.BlockSpec((seq, HEAD_DIM), lambda b, h: (b, group * nh + h))

    out_spec = pl.BlockSpec((seq, HEAD_DIM), lambda b, h: (b, h))
    out_shape = jax.ShapeDtypeStruct((t, nh * HEAD_DIM), BF16)
    return pl.pallas_call(
        _even_mixer_kernel,
        grid=(batch, nh),
        in_specs=[pl.BlockSpec(memory_space=pltpu.SMEM), pl.BlockSpec(memory_space=pltpu.SMEM),
                  head_spec(0), head_spec(1), head_spec(2), head_spec(3),
                  _resident((seq, HEAD_DIM)), _resident((seq, HEAD_DIM)),
                  pl.BlockSpec((1, HEAD_DIM), lambda b, h: (0, h)),
                  head_spec(4), head_spec(5), head_spec(6),
                  pl.BlockSpec((None, 1, 2 * blk), lambda b, h: (h, 0, 0))],
        out_specs=[out_spec, out_spec],
        out_shape=[out_shape, out_shape],
        scratch_shapes=[pltpu.VMEM((HEAD_DIM + BF16_SUBLANES, seq), BF16),
                        pltpu.VMEM((2, seq, blk), F32)],
        compiler_params=_params("parallel", "parallel"),
        name="even_mixer",
    )(log_g, b31, proj, proj, proj, proj, cos2, sin2, gn_w, proj, proj, proj, bvec)


def _fox_gate_kernel(fl_ref, bf_ref, o_ref):
    x = fl_ref[...] + bf_ref[...]
    y = jnp.minimum(x, 0.0) - jnp.log1p(jnp.exp(-jnp.abs(x)))
    n = y.shape[1]
    lane = lax.broadcasted_iota(jnp.int32, y.shape, 1)
    sh = 1
    while sh < n:
        y = y + jnp.where(lane >= sh, pltpu.roll(y, sh, 1), 0.0)
        sh *= 2
    o_ref[...] = y


def _fox_gate(fl_t, b_forget, batch, seq):
    nh = fl_t.shape[0]
    return pl.pallas_call(
        _fox_gate_kernel,
        grid=(batch,),
        in_specs=[pl.BlockSpec((nh, seq), lambda b: (0, b)), _resident((nh, 1))],
        out_specs=pl.BlockSpec((nh, seq), lambda b: (0, b)),
        out_shape=jax.ShapeDtypeStruct((nh, batch * seq), F32),
        compiler_params=_params("parallel"),
        name="fox_gate",
    )(fl_t, b_forget)


def _fox_kernel(q_ref, k_ref, v_ref, f_ref, o_ref, ka_ref, vt_ref, s_ref):
    blk = FOX_BLOCK
    nb = k_ref.shape[0] // blk
    d = HEAD_DIM
    lane = lax.broadcasted_iota(jnp.int32, (blk, d), 1)
    ones = jnp.where(lane < 3, 1.0, 0.0).astype(BF16)
    krow = lax.broadcasted_iota(jnp.int32, (blk, blk), 0)
    qcol = lax.broadcasted_iota(jnp.int32, (blk, blk), 1)
    causal = jnp.where(qcol >= krow, 0.0, NEG)
    pad = jnp.zeros((d - 8, blk), F32)

    def bias(i, j, aux):
        return causal if j == i else None

    problems = []
    for g in range(FOX_GROUP):
        cols = slice(g * d, (g + 1) * d)
        head = pl.program_id(1) * FOX_GROUP + g
        frow = f_ref[pl.ds(head, 1), :] * (-LOG2E)
        hi = frow.astype(BF16).astype(F32)
        mid = (frow - hi).astype(BF16).astype(F32)
        lo = frow - hi - mid
        sub = lax.broadcasted_iota(jnp.int32, (8, frow.shape[1]), 0)
        top = jnp.where(sub == 0, hi, jnp.where(sub == 1, mid, jnp.where(sub == 2, lo, 0.0)))
        for n in range(nb):
            rows = slice(n * blk, (n + 1) * blk)
            aug = jnp.transpose(jnp.concatenate([top[:, rows], pad], axis=0))
            ka_ref[g, rows, 0:d] = k_ref[rows, cols]
            ka_ref[g, rows, d:2 * d] = aug.astype(BF16)
        _fill_values(v_ref.at[:, cols], vt_ref.at[g], blk)

        def prep(i, q_head=q_ref.at[:, cols]):
            q = q_head[i * blk:(i + 1) * blk, :].astype(F32)
            return jnp.concatenate([(q * (d ** -0.5 * LOG2E)).astype(BF16), ones], axis=1), None

        problems.append(_AttnProblem(ka_ref.at[g], vt_ref.at[g], s_ref.at[g],
                                     o_ref.at[:, cols], prep, bias))

    _block_attention(nb, blk, FOX_ISSUE, problems)


def _fox(qkv, f, batch, seq):
    t = qkv.shape[0]
    nh = FOX_HEADS
    blk = FOX_BLOCK
    width = FOX_GROUP * HEAD_DIM
    groups = nh // FOX_GROUP

    def head_spec(group):
        return pl.BlockSpec((seq, width), lambda b, h: (b, group * groups + h))

    return pl.pallas_call(
        _fox_kernel,
        grid=(batch, groups),
        in_specs=[head_spec(0), head_spec(1), head_spec(2),
                  pl.BlockSpec((nh, seq), lambda b, h: (0, b))],
        out_specs=pl.BlockSpec((seq, width), lambda b, h: (b, h)),
        out_shape=jax.ShapeDtypeStruct((t, nh * HEAD_DIM), BF16),
        scratch_shapes=[pltpu.VMEM((FOX_GROUP, seq, 2 * HEAD_DIM), BF16),
                        pltpu.VMEM((FOX_GROUP, HEAD_DIM + BF16_SUBLANES, seq), BF16),
                        pltpu.VMEM((FOX_GROUP, 2, seq, blk), F32)],
        compiler_params=_params("parallel", "parallel"),
        name="fox",
    )(qkv, qkv, qkv, f)


def _rotary_tables(seq):
    inv_freq = np.power(ROPE_BASE, -np.arange(0, HEAD_DIM, 2, dtype=np.float64) / HEAD_DIM)
    ang = np.arange(seq, dtype=np.float64)[:, None] * inv_freq[None, :]
    cos, sin = np.cos(ang), np.sin(ang)
    return (jnp.asarray(np.concatenate([cos, cos], axis=-1), F32),
            jnp.asarray(np.concatenate([-sin, sin], axis=-1), F32))


def _t5_bucket(rel):
    n = jnp.maximum(rel, 0)
    max_exact = N_REL_BUCKETS // 2
    nf = jnp.maximum(n, 1).astype(F32)
    large = max_exact + (jnp.log(nf / max_exact) / math.log(REL_MAX_DISTANCE / max_exact)
                         * (N_REL_BUCKETS - max_exact)).astype(jnp.int32)
    large = jnp.minimum(large, N_REL_BUCKETS - 1)
    return jnp.where(n < max_exact, n, large)


def _moba_bias_vectors(rel_bias):
    blk = MOBA_BLOCK
    bias_t = rel_bias.astype(F32).T
    hot = _t5_bucket(jnp.arange(2 * blk))[:, None] == jnp.arange(N_REL_BUCKETS)
    near = jnp.sum(jnp.where(hot[None], bias_t[:, None, :], 0.0), axis=-1)
    far = bias_t[:, _t5_bucket(jnp.array(2 * blk, jnp.int32))]
    return far, near[:, None, :]


def kernel(x, rel_bias, norm_mix_pre, norm_mix_post, norm_ffn_pre, norm_ffn_post,
           even_w_in, even_gn, even_w_out, odd_w_in, odd_b_forget, odd_w_out,
           ffn_w_in, ffn_w_out):
    batch, seq, d = x.shape
    depth = norm_mix_pre.shape[0]
    xt = x.reshape(batch * seq, d)

    cos2, sin2 = _rotary_tables(seq)
    log_g = jnp.log1p(-jnp.power(2.0, -5.0 - jnp.arange(RET_HEADS, dtype=F32)))
    b31, bvec = _moba_bias_vectors(rel_bias)

    def row(v):
        return v.reshape(1, -1).astype(F32)

    w_ffn_all = ffn_w_in.astype(BF16)
    w_o_all = ffn_w_out.astype(BF16)
    for layer in range(depth):
        i = layer // 2
        if layer % 2 == 0:
            proj = _norm_proj(xt, row(norm_mix_pre[layer]), even_w_in[i].astype(BF16))
            ret, mob = _even_mixer(proj, log_g, cos2, sin2, row(even_gn[i]), b31, bvec,
                                   batch, seq)
            acts, w_mix = [ret, mob], even_w_out[i]
        else:
            n_qkv = 3 * FOX_HEADS * HEAD_DIM
            w_gate = jnp.pad(odd_w_in[i][:, n_qkv:], ((0, 0), (0, HEAD_DIM - FOX_HEADS)))
            qkv, fl_t = _norm_proj_gate(xt, row(norm_mix_pre[layer]),
                                        odd_w_in[i][:, :n_qkv].astype(BF16),
                                        w_gate.astype(BF16), FOX_HEADS)
            f = _fox_gate(fl_t, odd_b_forget[i].reshape(-1, 1).astype(F32), batch, seq)
            acts, w_mix = [_fox(qkv, f, batch, seq)], odd_w_out[i]
        xt = _mix_ffn(acts, w_mix.astype(BF16), row(norm_mix_post[layer]), xt,
                      row(norm_ffn_pre[layer]), w_ffn_all, w_o_all,
                      row(norm_ffn_post[layer]), layer)
    return xt.reshape(batch, seq, d)
```

```python
import functools
import math
from typing import Any, Callable, NamedTuple

import jax
import jax.numpy as jnp
import numpy as np
from jax import lax
from jax.experimental import pallas as pl
from jax.experimental.pallas import tpu as pltpu

F32 = jnp.float32
BF16 = jnp.bfloat16

HEAD_DIM = 128
RET_HEADS = 4
MOBA_HEADS = 4
FOX_HEADS = 8
RET_CHUNK = 128
MOBA_BLOCK = 256
MOBA_TOPK = 3
N_REL_BUCKETS = 32
REL_MAX_DISTANCE = 128
ROPE_BASE = 10000.0
RMS_EPS = 1e-6
GN_EPS = 1e-5
NEG = -1e30
LOG2E = math.log2(math.e)

V7X_VMEM_BYTES = 64 * 1024 * 1024
VMEM_LIMIT = V7X_VMEM_BYTES - 8 * 1024 * 1024

ROW_TILE = 1024
MIX_ROWS = 1024
SUB_ROWS = 512
COL_CHUNK = 512
FFN_CHUNK = 256
FOX_BLOCK = 256
FOX_GROUP = 2
BF16_SUBLANES = 16
MOBA_ISSUE = (2, 4)
RET_FILL_EVERY = 1
FOX_ISSUE = (4, 4)


def _params(*sem, flags=None):
    return pltpu.CompilerParams(dimension_semantics=sem, vmem_limit_bytes=VMEM_LIMIT,
                                flags=flags)


def _resident(shape):
    nd = len(shape)
    return pl.BlockSpec(shape, lambda *_: (0,) * nd, pipeline_mode=pl.Buffered(1))


def _rms(x, g):
    return x * lax.rsqrt(jnp.mean(x * x, axis=-1, keepdims=True) + RMS_EPS) * g


def _dot(a, b):
    return jnp.dot(a, b, preferred_element_type=F32)


def _dot_nt(a, b):
    return lax.dot_general(a, b, (((1,), (1,)), ((), ())), preferred_element_type=F32)


def _sub_tiles(rows, sub):
    return [slice(r, r + sub) for r in range(0, rows, sub)]


def _norm_proj_kernel(x_ref, g_ref, w_ref, o_ref):
    subs = _sub_tiles(x_ref.shape[0], SUB_ROWS)
    hns = [_rms(x_ref[rows, :], g_ref[...]).astype(BF16) for rows in subs]
    for rows, hn in zip(subs, hns):
        for c in range(o_ref.shape[1] // COL_CHUNK):
            sl = slice(c * COL_CHUNK, (c + 1) * COL_CHUNK)
            o_ref[rows, sl] = _dot(hn, w_ref[:, sl]).astype(o_ref.dtype)


def _norm_proj(x, g, w):
    t, d = x.shape
    n = w.shape[1]
    return pl.pallas_call(
        _norm_proj_kernel,
        grid=(t // ROW_TILE,),
        in_specs=[pl.BlockSpec((ROW_TILE, d), lambda i: (i, 0)),
                  _resident((1, d)), _resident((d, n))],
        out_specs=pl.BlockSpec((ROW_TILE, n), lambda i: (i, 0)),
        out_shape=jax.ShapeDtypeStruct((t, n), BF16),
        compiler_params=_params("parallel"),
        name="norm_proj",
    )(x, g, w)


def _norm_proj_gate_kernel(x_ref, g_ref, w_ref, wf_ref, o_ref, f_ref):
    n = o_ref.shape[1]
    subs = _sub_tiles(x_ref.shape[0], SUB_ROWS)
    hns = [_rms(x_ref[rows, :], g_ref[...]).astype(BF16) for rows in subs]
    for rows, hn in zip(subs, hns):
        for c in range(n // COL_CHUNK):
            sl = slice(c * COL_CHUNK, (c + 1) * COL_CHUNK)
            o_ref[rows, sl] = _dot(hn, w_ref[:, sl]).astype(o_ref.dtype)
        fl = _dot(hn, wf_ref[...])
        f_ref[:, rows] = jnp.transpose(fl)[0:f_ref.shape[0], :]


def _norm_proj_gate(x, g, w, wf, nh):
    t, d = x.shape
    n = w.shape[1]
    return pl.pallas_call(
        _norm_proj_gate_kernel,
        grid=(t // ROW_TILE,),
        in_specs=[pl.BlockSpec((ROW_TILE, d), lambda i: (i, 0)),
                  _resident((1, d)), _resident(w.shape), _resident(wf.shape)],
        out_specs=[pl.BlockSpec((ROW_TILE, n), lambda i: (i, 0)),
                   pl.BlockSpec((nh, ROW_TILE), lambda i: (0, i))],
        out_shape=[jax.ShapeDtypeStruct((t, n), BF16),
                   jax.ShapeDtypeStruct((nh, t), F32)],
        compiler_params=_params("parallel"),
        name="norm_proj_gate",
    )(x, g, w, wf)


def _mix_ffn_kernel(n_in, *refs):
    a_refs = refs[:n_in]
    wmix_ref, gmix_ref, x_ref, gpre_ref, wffn_ref, wo_ref, gpost_ref, o_ref = refs[n_in:]
    hidden = wo_ref.shape[0]
    subs = _sub_tiles(x_ref.shape[0], SUB_ROWS)

    xs, hns = [], []
    for rows in subs:
        m = None
        off = 0
        for a_ref in a_refs:
            width = a_ref.shape[1]
            part = _dot(a_ref[rows, :], wmix_ref[off:off + width, :])
            m = part if m is None else m + part
            off += width
        x = x_ref[rows, :] + _rms(m, gmix_ref[...])
        xs.append(x)
        hns.append(_rms(x, gpre_ref[...]).astype(BF16))
    for rows, x, hn in zip(subs, xs, hns):
        y = jnp.zeros(x.shape, F32)
        for c in range(hidden // FFN_CHUNK):
            lo = c * FFN_CHUNK
            gate = _dot(hn, wffn_ref[:, lo:lo + FFN_CHUNK])
            up = _dot(hn, wffn_ref[:, hidden + lo:hidden + lo + FFN_CHUNK])
            act = (gate * jax.nn.sigmoid(gate) * up).astype(BF16)
            y = y + _dot(act, wo_ref[lo:lo + FFN_CHUNK, :])
        o_ref[rows, :] = x + _rms(y, gpost_ref[...])


def _mix_ffn(acts, w_mix, g_mix, x, g_pre, w_ffn, w_o, g_post, layer):
    t, d = x.shape

    def layer_slab(w):
        return pl.BlockSpec((None,) + w.shape[1:], lambda i: (layer, 0, 0),
                            pipeline_mode=pl.Buffered(1))

    in_specs = [pl.BlockSpec((MIX_ROWS, a.shape[1]), lambda i: (i, 0)) for a in acts]
    in_specs += [_resident(w_mix.shape), _resident((1, d)),
                 pl.BlockSpec((MIX_ROWS, d), lambda i: (i, 0)),
                 _resident((1, d)), layer_slab(w_ffn), layer_slab(w_o),
                 _resident((1, d))]
    return pl.pallas_call(
        functools.partial(_mix_ffn_kernel, len(acts)),
        grid=(t // MIX_ROWS,),
        in_specs=in_specs,
        out_specs=pl.BlockSpec((MIX_ROWS, d), lambda i: (i, 0)),
        out_shape=jax.ShapeDtypeStruct((t, d), F32),
        compiler_params=_params("parallel"),
        name="mix_ffn",
    )(*acts, w_mix, g_mix, x, g_pre, w_ffn, w_o, g_post)


def _retention_work(lg, q_ref, k_ref, v_ref, g_ref, cos_ref, sin_ref, gn_ref, o_ref):
    c = RET_CHUNK
    seq = q_ref.shape[0]
    row = lax.broadcasted_iota(jnp.int32, (c, c), 0).astype(F32)
    col = lax.broadcasted_iota(jnp.int32, (c, c), 1).astype(F32)
    diff = row - col
    inner = jnp.where(diff >= 0, jnp.exp(lg * jnp.maximum(diff, 0.0)), 0.0)
    idx = lax.broadcasted_iota(jnp.int32, (c, HEAD_DIM), 0).astype(F32)
    q_dec = jnp.exp(lg * (idx + 1.0))
    k_dec = jnp.exp(lg * (c - 1.0 - idx))
    chunk_dec = jnp.exp(lg * jnp.full((1, HEAD_DIM), float(c), F32))
    gn_w = gn_ref[...]
    scale = HEAD_DIM ** -0.5

    def rot(t, cos, sin):
        return t * cos + pltpu.roll(t, HEAD_DIM // 2, 1) * sin

    def head(n):
        sl = slice(n * c, (n + 1) * c)
        cos = cos_ref[sl, :]
        sin = sin_ref[sl, :]
        q = rot(q_ref[sl, :].astype(F32), cos, sin) * scale
        k = rot(k_ref[sl, :].astype(F32), cos, sin)
        qb = q.astype(BF16)
        return qb, k, _dot_nt(qb, k.astype(BF16))

    def tail(n, state, qb, k, sc):
        sl = slice(n * c, (n + 1) * c)
        v = v_ref[sl, :]
        kd_t = jnp.transpose(k * k_dec).astype(BF16)
        kv = _dot(kd_t, v)
        cross = _dot(qb, state.astype(BF16))
        o = _dot((sc * inner).astype(BF16), v) + cross * q_dec
        mu = jnp.mean(o, axis=-1, keepdims=True)
        var = jnp.mean(jnp.square(o - mu), axis=-1, keepdims=True)
        y = (o - mu) * lax.rsqrt(var + GN_EPS) * gn_w
        gate = g_ref[sl, :].astype(F32)
        o_ref[sl, :] = (y * (gate * jax.nn.sigmoid(gate))).astype(o_ref.dtype)
        return state * chunk_dec + kv

    n_chunks = seq // c
    work = {"n": 0, "state": jnp.zeros((HEAD_DIM, HEAD_DIM), F32), "ahead": head(0)}

    def emit():
        n = work["n"]
        if n >= n_chunks:
            return False
        cur = work["ahead"]
        if n + 1 < n_chunks:
            work["ahead"] = head(n + 1)
        work["state"] = tail(n, work["state"], *cur)
        work["n"] = n + 1
        return n + 1 < n_chunks

    return emit


def _group_reduce(x, op):
    r, c = x.shape
    g = r // 8
    if g % 4 == 0:
        return op(op(x.reshape(4, g // 4, 8, c), axis=1), axis=0)
    return op(x.reshape(g, 8, c), axis=0)


class _AttnProblem(NamedTuple):
    keys: Any
    vt: Any
    scores: Any
    out: Any
    prep: Callable
    bias: Callable


def _block_attention(nb, blk, issue, problems, filler=None, fill_every=1):
    d = HEAD_DIM
    chunk, lag = issue
    steps = 0

    def chunks(i):
        return [(lo, min(lo + chunk, i + 1)) for lo in range(0, i + 1, chunk)]

    def score_chunk(p, i, lo, hi, qa, aux, m8, snaps):
        s = _dot_nt(p.keys[lo * blk:hi * blk, :], qa)
        for j in range(lo, hi):
            t = s[(j - lo) * blk:(j - lo + 1) * blk, :]
            tile_bias, col_bias = p.bias(i, j, aux)
            if tile_bias is not None:
                t = t + tile_bias
            p.scores[i % 2, j * blk:(j + 1) * blk, :] = t
            t8 = _group_reduce(t, jnp.max)
            if col_bias is not None:
                t8 = t8 + col_bias
            m8 = t8 if m8 is None else jnp.maximum(m8, t8)
            snaps[j] = m8
        return m8

    def attend_chunk(p, i, lo, hi, m, aux, acc, snaps):
        ps = []
        for j in range(lo, hi):
            snap = snaps.get(j - lag)
            m_j = m if snap is None else jnp.minimum(m, jnp.maximum(m, snap[0:1, :]))
            col_bias = p.bias(i, j, aux)[1]
            if col_bias is not None:
                m_j = m_j - col_bias
            ps.append(jnp.exp2(p.scores[i % 2, j * blk:(j + 1) * blk, :] - m_j).astype(BF16))
        pm = ps[0] if len(ps) == 1 else jnp.concatenate(ps, axis=0)
        part = _dot(p.vt[:, lo * blk:hi * blk], pm)
        return part if acc is None else acc + part

    states = []
    for p in problems:
        qa, aux = p.prep(nb - 1)
        m8 = None
        for lo, hi in chunks(nb - 1):
            m8 = score_chunk(p, nb - 1, lo, hi, qa, aux, m8, {})
        states.append({"m8": m8, "aux_next": aux})
    for i in reversed(range(nb)):
        ahead = chunks(i - 1) if i > 0 else []
        here = chunks(i)
        for p, st in zip(problems, states):
            st["m"] = jnp.max(st["m8"], axis=0, keepdims=True)
            st["aux"] = st["aux_next"]
            st["ops"] = p.prep(i - 1) if i > 0 else None
            st["aux_next"] = st["ops"][1] if i > 0 else None
            st["m8"], st["acc"], st["snaps"] = None, None, {}
        for c in range(len(here)):
            for p, st in zip(problems, states):
                done = dict(st["snaps"])
                if c < len(ahead):
                    st["m8"] = score_chunk(p, i - 1, *ahead[c], *st["ops"], st["m8"],
                                           st["snaps"])
                st["acc"] = attend_chunk(p, i, *here[c], st["m"], st["aux"], st["acc"], done)
            steps += 1
            if filler is not None and steps % fill_every == 0:
                filler()
        for p, st in zip(problems, states):
            acc = st["acc"]
            out = acc[0:d, :] / acc[d:d + 1, :]
            p.out[i * blk:(i + 1) * blk, :] = jnp.transpose(out).astype(p.out.dtype)
    while filler is not None and filler():
        pass


def _fill_values(v_ref, vt_ref, blk):
    d = HEAD_DIM
    seq = v_ref.shape[0]
    for n in range(seq // blk):
        rows = slice(n * blk, (n + 1) * blk)
        vt_ref[0:d, rows] = jnp.transpose(v_ref[rows, :].astype(F32)).astype(BF16)
    first = lax.broadcasted_iota(jnp.int32, (vt_ref.shape[0] - d, seq), 0) == 0
    vt_ref[d:, :] = jnp.where(first, 1.0, 0.0).astype(BF16)


def _even_mixer_kernel(lg_ref, b31_ref, rq_ref, rk_ref, rv_ref, rg_ref, cos_ref, sin_ref,
                       gn_ref, q_ref, k_ref, v_ref, bvec_ref, ret_ref, o_ref, vt_ref, s_ref):
    blk = MOBA_BLOCK
    nb = k_ref.shape[0] // blk
    h = pl.program_id(1)
    b31 = b31_ref[h]
    retention_chunk = _retention_work(lg_ref[h], rq_ref, rk_ref, rv_ref, rg_ref,
                                      cos_ref, sin_ref, gn_ref, ret_ref)

    _fill_values(v_ref, vt_ref, blk)
    means = [jnp.mean(k_ref[n * blk:(n + 1) * blk, :].astype(F32), axis=0, keepdims=True)
             for n in range(nb)]
    km = jnp.concatenate(means, axis=0)
    km_hi = km.astype(BF16)
    km_lo = (km - km_hi.astype(F32)).astype(BF16)
    row = lax.broadcasted_iota(jnp.int32, (nb, blk), 0).astype(F32)
    bvec = jnp.broadcast_to(bvec_ref[...] * LOG2E, (blk, 2 * blk))
    toep = pltpu.roll(bvec, 0, 1, stride=1, stride_axis=0)
    krow = lax.broadcasted_iota(jnp.int32, (blk, blk), 0)
    qcol = lax.broadcasted_iota(jnp.int32, (blk, blk), 1)
    town = jnp.where(qcol >= krow, toep[:, 0:blk], NEG)
    tadj = toep[:, blk:2 * blk]

    def prep(i):
        q = q_ref[i * blk:(i + 1) * blk, :]
        gate = _dot_nt(km_hi, q) + _dot_nt(km_lo, q)
        valid = row < float(i)
        gm = jnp.where(valid, gate, -jnp.inf)
        sel = jnp.full(gate.shape, NEG, F32)
        for _ in range(min(MOBA_TOPK, nb)):
            mx = jnp.max(gm, axis=0, keepdims=True)
            first = jnp.min(jnp.where(gm == mx, row, float(nb)), axis=0, keepdims=True)
            pick = row == first
            sel = jnp.where(jnp.logical_and(pick, valid), 0.0, sel)
            gm = jnp.where(pick, -jnp.inf, gm)

        qs = (q.astype(F32) * (HEAD_DIM ** -0.5 * LOG2E)).astype(BF16)
        return qs, (sel, sel + b31 * LOG2E)

    def bias(i, j, aux):
        sel, far = aux
        if j == i:
            return town, None
        if j == i - 1:
            return tadj, sel[j:j + 1, :]
        return None, far[j:j + 1, :]

    _block_attention(nb, blk, MOBA_ISSUE,
                     [_AttnProblem(k_ref, vt_ref, s_ref, o_ref, prep, bias)],
                     filler=retention_chunk, fill_every=RET_FILL_EVERY)


def _even_mixer(proj, log_g, cos2, sin2, gn_w, b31, bvec, batch, seq):
    assert RET_HEADS == MOBA_HEADS
    t = proj.shape[0]
    nh = MOBA_HEADS
    blk = MOBA_BLOCK

    def head_spec(group):
        return pl.BlockSpec((seq, HEAD_DIM), lambda b, h: (b, group * nh + h))

    out_spec = pl.BlockSpec((seq, HEAD_DIM), lambda b, h: (b, h))
    out_shape = jax.ShapeDtypeStruct((t, nh * HEAD_DIM), BF16)
    return pl.pallas_call(
        _even_mixer_kernel,
        grid=(batch, nh),
        in_specs=[pl.BlockSpec(memory_space=pltpu.SMEM), pl.BlockSpec(memory_space=pltpu.SMEM),
                  head_spec(0), head_spec(1), head_spec(2), head_spec(3),
                  _resident((seq, HEAD_DIM)), _resident((seq, HEAD_DIM)),
                  pl.BlockSpec((1, HEAD_DIM), lambda b, h: (0, h)),
                  head_spec(4), head_spec(5), head_spec(6),
                  pl.BlockSpec((None, 1, 2 * blk), lambda b, h: (h, 0, 0))],
        out_specs=[out_spec, out_spec],
        out_shape=[out_shape, out_shape],
        scratch_shapes=[pltpu.VMEM((HEAD_DIM + BF16_SUBLANES, seq), BF16),
                        pltpu.VMEM((2, seq, blk), F32)],
        compiler_params=_params("parallel", "parallel"),
        name="even_mixer",
    )(log_g, b31, proj, proj, proj, proj, cos2, sin2, gn_w, proj, proj, proj, bvec)


def _fox_gate_kernel(fl_ref, bf_ref, o_ref):
    x = fl_ref[...] + bf_ref[...]
    y = jnp.minimum(x, 0.0) - jnp.log1p(jnp.exp(-jnp.abs(x)))
    n = y.shape[1]
    lane = lax.broadcasted_iota(jnp.int32, y.shape, 1)
    sh = 1
    while sh < n:
        y = y + jnp.where(lane >= sh, pltpu.roll(y, sh, 1), 0.0)
        sh *= 2
    o_ref[...] = y


def _fox_gate(fl_t, b_forget, batch, seq):
    nh = fl_t.shape[0]
    return pl.pallas_call(
        _fox_gate_kernel,
        grid=(batch,),
        in_specs=[pl.BlockSpec((nh, seq), lambda b: (0, b)), _resident((nh, 1))],
        out_specs=pl.BlockSpec((nh, seq), lambda b: (0, b)),
        out_shape=jax.ShapeDtypeStruct((nh, batch * seq), F32),
        compiler_params=_params("parallel"),
        name="fox_gate",
    )(fl_t, b_forget)


def _fox_kernel(q_ref, k_ref, v_ref, f_ref, o_ref, ka_ref, vt_ref, s_ref):
    blk = FOX_BLOCK
    nb = k_ref.shape[0] // blk
    d = HEAD_DIM
    lane = lax.broadcasted_iota(jnp.int32, (blk, d), 1)
    ones = jnp.where(lane < 3, 1.0, 0.0).astype(BF16)
    krow = lax.broadcasted_iota(jnp.int32, (blk, blk), 0)
    qcol = lax.broadcasted_iota(jnp.int32, (blk, blk), 1)
    causal = jnp.where(qcol >= krow, 0.0, NEG)
    pad = jnp.zeros((d - 8, blk), F32)

    def bias(i, j, aux):
        return (causal if j == i else None), None

    problems = []
    for g in range(FOX_GROUP):
        cols = slice(g * d, (g + 1) * d)
        head = pl.program_id(1) * FOX_GROUP + g
        frow = f_ref[pl.ds(head, 1), :] * (-LOG2E)
        hi = frow.astype(BF16).astype(F32)
        mid = (frow - hi).astype(BF16).astype(F32)
        lo = frow - hi - mid
        sub = lax.broadcasted_iota(jnp.int32, (8, frow.shape[1]), 0)
        top = jnp.where(sub == 0, hi, jnp.where(sub == 1, mid, jnp.where(sub == 2, lo, 0.0)))
        for n in range(nb):
            rows = slice(n * blk, (n + 1) * blk)
            aug = jnp.transpose(jnp.concatenate([top[:, rows], pad], axis=0))
            ka_ref[g, rows, 0:d] = k_ref[rows, cols]
            ka_ref[g, rows, d:2 * d] = aug.astype(BF16)
        _fill_values(v_ref.at[:, cols], vt_ref.at[g], blk)

        def prep(i, q_head=q_ref.at[:, cols]):
            q = q_head[i * blk:(i + 1) * blk, :].astype(F32)
            return jnp.concatenate([(q * (d ** -0.5 * LOG2E)).astype(BF16), ones], axis=1), None

        problems.append(_AttnProblem(ka_ref.at[g], vt_ref.at[g], s_ref.at[g],
                                     o_ref.at[:, cols], prep, bias))

    _block_attention(nb, blk, FOX_ISSUE, problems)


def _fox(qkv, f, batch, seq):
    t = qkv.shape[0]
    nh = FOX_HEADS
    blk = FOX_BLOCK
    width = FOX_GROUP * HEAD_DIM
    groups = nh // FOX_GROUP

    def head_spec(group):
        return pl.BlockSpec((seq, width), lambda b, h: (b, group * groups + h))

    return pl.pallas_call(
        _fox_kernel,
        grid=(batch, groups),
        in_specs=[head_spec(0), head_spec(1), head_spec(2),
                  pl.BlockSpec((nh, seq), lambda b, h: (0, b))],
        out_specs=pl.BlockSpec((seq, width), lambda b, h: (b, h)),
        out_shape=jax.ShapeDtypeStruct((t, nh * HEAD_DIM), BF16),
        scratch_shapes=[pltpu.VMEM((FOX_GROUP, seq, 2 * HEAD_DIM), BF16),
                        pltpu.VMEM((FOX_GROUP, HEAD_DIM + BF16_SUBLANES, seq), BF16),
                        pltpu.VMEM((FOX_GROUP, 2, seq, blk), F32)],
        compiler_params=_params("parallel", "parallel"),
        name="fox",
    )(qkv, qkv, qkv, f)


def _rotary_tables(seq):
    inv_freq = np.power(ROPE_BASE, -np.arange(0, HEAD_DIM, 2, dtype=np.float64) / HEAD_DIM)
    ang = np.arange(seq, dtype=np.float64)[:, None] * inv_freq[None, :]
    cos, sin = np.cos(ang), np.sin(ang)
    return (jnp.asarray(np.concatenate([cos, cos], axis=-1), F32),
            jnp.asarray(np.concatenate([-sin, sin], axis=-1), F32))


def _t5_bucket(rel):
    n = jnp.maximum(rel, 0)
    max_exact = N_REL_BUCKETS // 2
    nf = jnp.maximum(n, 1).astype(F32)
    large = max_exact + (jnp.log(nf / max_exact) / math.log(REL_MAX_DISTANCE / max_exact)
                         * (N_REL_BUCKETS - max_exact)).astype(jnp.int32)
    large = jnp.minimum(large, N_REL_BUCKETS - 1)
    return jnp.where(n < max_exact, n, large)


def _moba_bias_vectors(rel_bias):
    blk = MOBA_BLOCK
    bias_t = rel_bias.astype(F32).T
    hot = _t5_bucket(jnp.arange(2 * blk))[:, None] == jnp.arange(N_REL_BUCKETS)
    near = jnp.sum(jnp.where(hot[None], bias_t[:, None, :], 0.0), axis=-1)
    far = bias_t[:, _t5_bucket(jnp.array(2 * blk, jnp.int32))]
    return far, near[:, None, :]


def kernel(x, rel_bias, norm_mix_pre, norm_mix_post, norm_ffn_pre, norm_ffn_post,
           even_w_in, even_gn, even_w_out, odd_w_in, odd_b_forget, odd_w_out,
           ffn_w_in, ffn_w_out):
    batch, seq, d = x.shape
    depth = norm_mix_pre.shape[0]
    xt = x.reshape(batch * seq, d)

    cos2, sin2 = _rotary_tables(seq)
    log_g = jnp.log1p(-jnp.power(2.0, -5.0 - jnp.arange(RET_HEADS, dtype=F32)))
    b31, bvec = _moba_bias_vectors(rel_bias)

    def row(v):
        return v.reshape(1, -1).astype(F32)

    w_ffn_all = ffn_w_in.astype(BF16)
    w_o_all = ffn_w_out.astype(BF16)
    for layer in range(depth):
        i = layer // 2
        if layer % 2 == 0:
            proj = _norm_proj(xt, row(norm_mix_pre[layer]), even_w_in[i].astype(BF16))
            ret, mob = _even_mixer(proj, log_g, cos2, sin2, row(even_gn[i]), b31, bvec,
                                   batch, seq)
            acts, w_mix = [ret, mob], even_w_out[i]
        else:
            n_qkv = 3 * FOX_HEADS * HEAD_DIM
            w_gate = jnp.pad(odd_w_in[i][:, n_qkv:], ((0, 0), (0, HEAD_DIM - FOX_HEADS)))
            qkv, fl_t = _norm_proj_gate(xt, row(norm_mix_pre[layer]),
                                        odd_w_in[i][:, :n_qkv].astype(BF16),
                                        w_gate.astype(BF16), FOX_HEADS)
            f = _fox_gate(fl_t, odd_b_forget[i].reshape(-1, 1).astype(F32), batch, seq)
            acts, w_mix = [_fox(qkv, f, batch, seq)], odd_w_out[i]
        xt = _mix_ffn(acts, w_mix.astype(BF16), row(norm_mix_post[layer]), xt,
                      row(norm_ffn_pre[layer]), w_ffn_all, w_o_all,
                      row(norm_ffn_post[layer]), layer)
    return xt.reshape(batch, seq, d)
```

```python
import functools
import math
from typing import Any, Callable, NamedTuple

import jax
import jax.numpy as jnp
import numpy as np
from jax import lax
from jax.experimental import pallas as pl
from jax.experimental.pallas import tpu as pltpu

F32 = jnp.float32
BF16 = jnp.bfloat16

HEAD_DIM = 128
RET_HEADS = 4
MOBA_HEADS = 4
FOX_HEADS = 8
RET_CHUNK = 128
MOBA_BLOCK = 256
MOBA_TOPK = 3
N_REL_BUCKETS = 32
REL_MAX_DISTANCE = 128
ROPE_BASE = 10000.0
RMS_EPS = 1e-6
GN_EPS = 1e-5
NEG = -1e30
LOG2E = math.log2(math.e)

V7X_VMEM_BYTES = 64 * 1024 * 1024
VMEM_LIMIT = V7X_VMEM_BYTES - 8 * 1024 * 1024

F32_SUBLANES = 8
BF16_SUBLANES = 16

ROW_TILE = 1024
MIX_ROWS = 1024
SUB_ROWS = 512
COL_CHUNK = 512
FFN_CHUNK = 256
FOX_BLOCK = 256
FOX_GROUP = 2
FOX_BIAS_TERMS = 3
MOBA_ISSUE = (2, 4)
FOX_ISSUE = (4, 4)
RET_FILL_EVERY = 1


def _params(*sem):
    return pltpu.CompilerParams(dimension_semantics=sem, vmem_limit_bytes=VMEM_LIMIT)


def _resident(shape):
    nd = len(shape)
    return pl.BlockSpec(shape, lambda *_: (0,) * nd, pipeline_mode=pl.Buffered(1))


def _rms(x, g):
    return x * lax.rsqrt(jnp.mean(x * x, axis=-1, keepdims=True) + RMS_EPS) * g


def _dot(a, b):
    return jnp.dot(a, b, preferred_element_type=F32)


def _dot_nt(a, b):
    return lax.dot_general(a, b, (((1,), (1,)), ((), ())), preferred_element_type=F32)


def _sub_tiles(rows, sub):
    return [slice(r, r + sub) for r in range(0, rows, sub)]


def _norm_proj_kernel(x_ref, g_ref, w_ref, o_ref):
    subs = _sub_tiles(x_ref.shape[0], SUB_ROWS)
    hns = [_rms(x_ref[rows, :], g_ref[...]).astype(BF16) for rows in subs]
    for rows, hn in zip(subs, hns):
        for c in range(o_ref.shape[1] // COL_CHUNK):
            sl = slice(c * COL_CHUNK, (c + 1) * COL_CHUNK)
            o_ref[rows, sl] = _dot(hn, w_ref[:, sl]).astype(o_ref.dtype)


def _norm_proj(x, g, w):
    t, d = x.shape
    n = w.shape[1]
    return pl.pallas_call(
        _norm_proj_kernel,
        grid=(t // ROW_TILE,),
        in_specs=[pl.BlockSpec((ROW_TILE, d), lambda i: (i, 0)),
                  _resident((1, d)), _resident((d, n))],
        out_specs=pl.BlockSpec((ROW_TILE, n), lambda i: (i, 0)),
        out_shape=jax.ShapeDtypeStruct((t, n), BF16),
        compiler_params=_params("parallel"),
        name="norm_proj",
    )(x, g, w)


def _norm_proj_gate_kernel(x_ref, g_ref, w_ref, wf_ref, o_ref, f_ref):
    n = o_ref.shape[1]
    subs = _sub_tiles(x_ref.shape[0], SUB_ROWS)
    hns = [_rms(x_ref[rows, :], g_ref[...]).astype(BF16) for rows in subs]
    for rows, hn in zip(subs, hns):
        for c in range(n // COL_CHUNK):
            sl = slice(c * COL_CHUNK, (c + 1) * COL_CHUNK)
            o_ref[rows, sl] = _dot(hn, w_ref[:, sl]).astype(o_ref.dtype)
        fl = _dot(hn, wf_ref[...])
        f_ref[:, rows] = jnp.transpose(fl)[0:f_ref.shape[0], :]


def _norm_proj_gate(x, g, w, wf, n, nh):
    t, d = x.shape
    return pl.pallas_call(
        _norm_proj_gate_kernel,
        grid=(t // ROW_TILE,),
        in_specs=[pl.BlockSpec((ROW_TILE, d), lambda i: (i, 0)),
                  _resident((1, d)), _resident(w.shape), _resident(wf.shape)],
        out_specs=[pl.BlockSpec((ROW_TILE, n), lambda i: (i, 0)),
                   pl.BlockSpec((nh, ROW_TILE), lambda i: (0, i))],
        out_shape=[jax.ShapeDtypeStruct((t, n), BF16),
                   jax.ShapeDtypeStruct((nh, t), F32)],
        compiler_params=_params("parallel"),
        name="norm_proj_gate",
    )(x, g, w, wf)


def _mix_ffn_kernel(n_in, *refs):
    a_refs = refs[:n_in]
    wmix_ref, gmix_ref, x_ref, gpre_ref, wffn_ref, wo_ref, gpost_ref, o_ref = refs[n_in:]
    hidden = wo_ref.shape[0]
    subs = _sub_tiles(x_ref.shape[0], SUB_ROWS)

    xs, hns = [], []
    for rows in subs:
        m = None
        off = 0
        for a_ref in a_refs:
            width = a_ref.shape[1]
            part = _dot(a_ref[rows, :], wmix_ref[off:off + width, :])
            m = part if m is None else m + part
            off += width
        x = x_ref[rows, :] + _rms(m, gmix_ref[...])
        xs.append(x)
        hns.append(_rms(x, gpre_ref[...]).astype(BF16))
    for rows, x, hn in zip(subs, xs, hns):
        y = jnp.zeros(x.shape, F32)
        for c in range(hidden // FFN_CHUNK):
            lo = c * FFN_CHUNK
            gate = _dot(hn, wffn_ref[:, lo:lo + FFN_CHUNK])
            up = _dot(hn, wffn_ref[:, hidden + lo:hidden + lo + FFN_CHUNK])
            act = (gate * jax.nn.sigmoid(gate) * up).astype(BF16)
            y = y + _dot(act, wo_ref[lo:lo + FFN_CHUNK, :])
        o_ref[rows, :] = x + _rms(y, gpost_ref[...])


def _mix_ffn(acts, w_mix, g_mix, x, g_pre, w_ffn, w_o, g_post, layer):
    t, d = x.shape

    def layer_slab(w):
        return pl.BlockSpec((None,) + w.shape[1:], lambda i: (layer, 0, 0),
                            pipeline_mode=pl.Buffered(1))

    in_specs = [pl.BlockSpec((MIX_ROWS, a.shape[1]), lambda i: (i, 0)) for a in acts]
    in_specs += [_resident(w_mix.shape), _resident((1, d)),
                 pl.BlockSpec((MIX_ROWS, d), lambda i: (i, 0)),
                 _resident((1, d)), layer_slab(w_ffn), layer_slab(w_o),
                 _resident((1, d))]
    return pl.pallas_call(
        functools.partial(_mix_ffn_kernel, len(acts)),
        grid=(t // MIX_ROWS,),
        in_specs=in_specs,
        out_specs=pl.BlockSpec((MIX_ROWS, d), lambda i: (i, 0)),
        out_shape=jax.ShapeDtypeStruct((t, d), F32),
        compiler_params=_params("parallel"),
        name="mix_ffn",
    )(*acts, w_mix, g_mix, x, g_pre, w_ffn, w_o, g_post)


def _retention_work(lg, q_ref, k_ref, v_ref, g_ref, cos_ref, sin_ref, gn_ref, o_ref):
    c = RET_CHUNK
    seq = q_ref.shape[0]
    row = lax.broadcasted_iota(jnp.int32, (c, c), 0).astype(F32)
    col = lax.broadcasted_iota(jnp.int32, (c, c), 1).astype(F32)
    diff = row - col
    inner = jnp.where(diff >= 0, jnp.exp(lg * jnp.maximum(diff, 0.0)), 0.0)
    idx = lax.broadcasted_iota(jnp.int32, (c, HEAD_DIM), 0).astype(F32)
    q_dec = jnp.exp(lg * (idx + 1.0))
    k_dec = jnp.exp(lg * (c - 1.0 - idx))
    chunk_dec = jnp.exp(lg * jnp.full((1, HEAD_DIM), float(c), F32))
    gn_w = gn_ref[...]
    scale = HEAD_DIM ** -0.5

    def rot(t, cos, sin):
        return t * cos + pltpu.roll(t, HEAD_DIM // 2, 1) * sin

    def head(n):
        sl = slice(n * c, (n + 1) * c)
        cos = cos_ref[sl, :]
        sin = sin_ref[sl, :]
        q = rot(q_ref[sl, :].astype(F32), cos, sin) * scale
        k = rot(k_ref[sl, :].astype(F32), cos, sin)
        qb = q.astype(BF16)
        return qb, k, _dot_nt(qb, k.astype(BF16))

    def tail(n, state, qb, k, sc):
        sl = slice(n * c, (n + 1) * c)
        v = v_ref[sl, :]
        kd_t = jnp.transpose(k * k_dec).astype(BF16)
        kv = _dot(kd_t, v)
        cross = _dot(qb, state.astype(BF16))
        o = _dot((sc * inner).astype(BF16), v) + cross * q_dec
        mu = jnp.mean(o, axis=-1, keepdims=True)
        var = jnp.mean(jnp.square(o - mu), axis=-1, keepdims=True)
        y = (o - mu) * lax.rsqrt(var + GN_EPS) * gn_w
        gate = g_ref[sl, :].astype(F32)
        o_ref[sl, :] = (y * (gate * jax.nn.sigmoid(gate))).astype(o_ref.dtype)
        return state * chunk_dec + kv

    n_chunks = seq // c
    work = {"n": 0, "state": jnp.zeros((HEAD_DIM, HEAD_DIM), F32), "ahead": head(0)}

    def emit():
        n = work["n"]
        if n >= n_chunks:
            return False
        cur = work["ahead"]
        if n + 1 < n_chunks:
            work["ahead"] = head(n + 1)
        work["state"] = tail(n, work["state"], *cur)
        work["n"] = n + 1
        return n + 1 < n_chunks

    return emit


def _group_reduce(x, op):
    r, c = x.shape
    s = F32_SUBLANES
    g = r // s
    if g % 4 == 0:
        return op(op(x.reshape(4, g // 4, s, c), axis=1), axis=0)
    return op(x.reshape(g, s, c), axis=0)


class _AttnProblem(NamedTuple):
    keys: Any
    vt: Any
    scores: Any
    out: Any
    prep: Callable
    bias: Callable


def _block_attention(nb, blk, issue, problems, filler=None, fill_every=1):
    d = HEAD_DIM
    chunk, lag = issue
    steps = 0

    def chunks(i):
        return [(lo, min(lo + chunk, i + 1)) for lo in range(0, i + 1, chunk)]

    def score_chunk(p, i, lo, hi, qa, aux, m8, snaps):
        s = _dot_nt(p.keys[lo * blk:hi * blk, :], qa)
        for j in range(lo, hi):
            t = s[(j - lo) * blk:(j - lo + 1) * blk, :]
            tile_bias, col_bias = p.bias(i, j, aux)
            if tile_bias is not None:
                t = t + tile_bias
            p.scores[i % 2, j * blk:(j + 1) * blk, :] = t
            t8 = _group_reduce(t, jnp.max)
            if col_bias is not None:
                t8 = t8 + col_bias
            m8 = t8 if m8 is None else jnp.maximum(m8, t8)
            snaps[j] = m8
        return m8

    def attend_chunk(p, i, lo, hi, m, aux, acc, snaps):
        ps = []
        for j in range(lo, hi):
            snap = snaps.get(j - lag)
            m_j = m if snap is None else jnp.minimum(m, jnp.maximum(m, snap[0:1, :]))
            col_bias = p.bias(i, j, aux)[1]
            if col_bias is not None:
                m_j = m_j - col_bias
            ps.append(jnp.exp2(p.scores[i % 2, j * blk:(j + 1) * blk, :] - m_j).astype(BF16))
        pm = ps[0] if len(ps) == 1 else jnp.concatenate(ps, axis=0)
        part = _dot(p.vt[:, lo * blk:hi * blk], pm)
        return part if acc is None else acc + part

    states = []
    for p in problems:
        qa, aux = p.prep(nb - 1)
        m8 = None
        for lo, hi in chunks(nb - 1):
            m8 = score_chunk(p, nb - 1, lo, hi, qa, aux, m8, {})
        states.append({"m8": m8, "aux_next": aux})
    for i in reversed(range(nb)):
        ahead = chunks(i - 1) if i > 0 else []
        here = chunks(i)
        for p, st in zip(problems, states):
            st["m"] = jnp.max(st["m8"], axis=0, keepdims=True)
            st["aux"] = st["aux_next"]
            st["ops"] = p.prep(i - 1) if i > 0 else None
            st["aux_next"] = st["ops"][1] if i > 0 else None
            st["m8"], st["acc"], st["snaps"] = None, None, {}
        for c in range(len(here)):
            for p, st in zip(problems, states):
                done = dict(st["snaps"])
                if c < len(ahead):
                    st["m8"] = score_chunk(p, i - 1, *ahead[c], *st["ops"], st["m8"],
                                           st["snaps"])
                st["acc"] = attend_chunk(p, i, *here[c], st["m"], st["aux"], st["acc"], done)
            steps += 1
            if filler is not None and steps % fill_every == 0:
                filler()
        for p, st in zip(problems, states):
            acc = st["acc"]
            out = acc[0:d, :] / acc[d:d + 1, :]
            p.out[i * blk:(i + 1) * blk, :] = jnp.transpose(out).astype(p.out.dtype)
    while filler is not None and filler():
        pass


def _fill_values(v_ref, vt_ref, blk):
    d = HEAD_DIM
    seq = v_ref.shape[0]
    for n in range(seq // blk):
        rows = slice(n * blk, (n + 1) * blk)
        vt_ref[0:d, rows] = jnp.transpose(v_ref[rows, :])
    first = lax.broadcasted_iota(jnp.int32, (vt_ref.shape[0] - d, seq), 0) == 0
    vt_ref[d:, :] = jnp.where(first, 1.0, 0.0).astype(BF16)


def _even_mixer_kernel(lg_ref, b31_ref, rq_ref, rk_ref, rv_ref, rg_ref, cos_ref, sin_ref,
                       gn_ref, q_ref, k_ref, v_ref, bvec_ref, ret_ref, o_ref, vt_ref, s_ref):
    blk = MOBA_BLOCK
    nb = k_ref.shape[0] // blk
    h = pl.program_id(1)
    b31 = b31_ref[h]
    retention_chunk = _retention_work(lg_ref[h], rq_ref, rk_ref, rv_ref, rg_ref,
                                      cos_ref, sin_ref, gn_ref, ret_ref)

    _fill_values(v_ref, vt_ref, blk)
    means = [jnp.mean(k_ref[n * blk:(n + 1) * blk, :].astype(F32), axis=0, keepdims=True)
             for n in range(nb)]
    km = jnp.concatenate(means, axis=0)
    km_hi = km.astype(BF16)
    km_lo = (km - km_hi.astype(F32)).astype(BF16)
    row = lax.broadcasted_iota(jnp.int32, (nb, blk), 0).astype(F32)
    bvec = jnp.broadcast_to(bvec_ref[...] * LOG2E, (blk, 2 * blk))
    toep = pltpu.roll(bvec, 0, 1, stride=1, stride_axis=0)
    krow = lax.broadcasted_iota(jnp.int32, (blk, blk), 0)
    qcol = lax.broadcasted_iota(jnp.int32, (blk, blk), 1)
    town = jnp.where(qcol >= krow, toep[:, 0:blk], NEG)
    tadj = toep[:, blk:2 * blk]

    def prep(i):
        q = q_ref[i * blk:(i + 1) * blk, :]
        gate = _dot_nt(km_hi, q) + _dot_nt(km_lo, q)
        valid = row < float(i)
        gm = jnp.where(valid, gate, -jnp.inf)
        sel = jnp.full(gate.shape, NEG, F32)
        for _ in range(min(MOBA_TOPK, nb)):
            mx = jnp.max(gm, axis=0, keepdims=True)
            first = jnp.min(jnp.where(gm == mx, row, float(nb)), axis=0, keepdims=True)
            pick = row == first
            sel = jnp.where(jnp.logical_and(pick, valid), 0.0, sel)
            gm = jnp.where(pick, -jnp.inf, gm)

        qs = (q.astype(F32) * (HEAD_DIM ** -0.5 * LOG2E)).astype(BF16)
        return qs, (sel, sel + b31 * LOG2E)

    def bias(i, j, aux):
        sel, far = aux
        if j == i:
            return town, None
        if j == i - 1:
            return tadj, sel[j:j + 1, :]
        return None, far[j:j + 1, :]

    _block_attention(nb, blk, MOBA_ISSUE,
                     [_AttnProblem(k_ref, vt_ref, s_ref, o_ref, prep, bias)],
                     filler=retention_chunk, fill_every=RET_FILL_EVERY)


def _even_mixer(proj, log_g, cos2, sin2, gn_w, b31, bvec, batch, seq):
    assert RET_HEADS == MOBA_HEADS
    t = proj.shape[0]
    nh = MOBA_HEADS
    blk = MOBA_BLOCK

    def head_spec(group):
        return pl.BlockSpec((seq, HEAD_DIM), lambda b, h: (b, group * nh + h))

    out_spec = pl.BlockSpec((seq, HEAD_DIM), lambda b, h: (b, h))
    out_shape = jax.ShapeDtypeStruct((t, nh * HEAD_DIM), BF16)
    return pl.pallas_call(
        _even_mixer_kernel,
        grid=(batch, nh),
        in_specs=[pl.BlockSpec(memory_space=pltpu.SMEM), pl.BlockSpec(memory_space=pltpu.SMEM),
                  head_spec(0), head_spec(1), head_spec(2), head_spec(3),
                  _resident((seq, HEAD_DIM)), _resident((seq, HEAD_DIM)),
                  pl.BlockSpec((1, HEAD_DIM), lambda b, h: (0, h)),
                  head_spec(4), head_spec(5), head_spec(6),
                  pl.BlockSpec((None, 1, 2 * blk), lambda b, h: (h, 0, 0))],
        out_specs=[out_spec, out_spec],
        out_shape=[out_shape, out_shape],
        scratch_shapes=[pltpu.VMEM((HEAD_DIM + BF16_SUBLANES, seq), BF16),
                        pltpu.VMEM((2, seq, blk), F32)],
        compiler_params=_params("parallel", "parallel"),
        name="even_mixer",
    )(log_g, b31, proj, proj, proj, proj, cos2, sin2, gn_w, proj, proj, proj, bvec)


def _fox_gate_kernel(fl_ref, bf_ref, o_ref):
    x = fl_ref[...] + bf_ref[...]
    y = jnp.minimum(x, 0.0) - jnp.log1p(jnp.exp(-jnp.abs(x)))
    n = y.shape[1]
    lane = lax.broadcasted_iota(jnp.int32, y.shape, 1)
    sh = 1
    while sh < n:
        y = y + jnp.where(lane >= sh, pltpu.roll(y, sh, 1), 0.0)
        sh *= 2
    o_ref[...] = y


def _fox_gate(fl_t, b_forget, batch, seq):
    nh = fl_t.shape[0]
    return pl.pallas_call(
        _fox_gate_kernel,
        grid=(batch,),
        in_specs=[pl.BlockSpec((nh, seq), lambda b: (0, b)), _resident((nh, 1))],
        out_specs=pl.BlockSpec((nh, seq), lambda b: (0, b)),
        out_shape=jax.ShapeDtypeStruct((nh, batch * seq), F32),
        compiler_params=_params("parallel"),
        name="fox_gate",
    )(fl_t, b_forget)


def _fox_kernel(q_ref, k_ref, v_ref, f_ref, o_ref, ka_ref, vt_ref, s_ref):
    blk = FOX_BLOCK
    nb = k_ref.shape[0] // blk
    d = HEAD_DIM
    lane = lax.broadcasted_iota(jnp.int32, (blk, d), 1)
    ones = jnp.where(lane < FOX_BIAS_TERMS, 1.0, 0.0).astype(BF16)
    krow = lax.broadcasted_iota(jnp.int32, (blk, blk), 0)
    qcol = lax.broadcasted_iota(jnp.int32, (blk, blk), 1)
    causal = jnp.where(qcol >= krow, 0.0, NEG)
    pad = jnp.zeros((d - F32_SUBLANES, blk), F32)

    def bias(i, j, aux):
        return (causal if j == i else None), None

    problems = []
    for g in range(FOX_GROUP):
        cols = slice(g * d, (g + 1) * d)
        head = pl.program_id(1) * FOX_GROUP + g
        frow = f_ref[pl.ds(head, 1), :] * (-LOG2E)
        hi = frow.astype(BF16).astype(F32)
        mid = (frow - hi).astype(BF16).astype(F32)
        lo = frow - hi - mid
        sub = lax.broadcasted_iota(jnp.int32, (F32_SUBLANES, frow.shape[1]), 0)
        top = jnp.where(sub == 0, hi, jnp.where(sub == 1, mid, jnp.where(sub == 2, lo, 0.0)))
        for n in range(nb):
            rows = slice(n * blk, (n + 1) * blk)
            aug = jnp.transpose(jnp.concatenate([top[:, rows], pad], axis=0))
            ka_ref[g, rows, 0:d] = k_ref[rows, cols]
            ka_ref[g, rows, d:2 * d] = aug.astype(BF16)
        _fill_values(v_ref.at[:, cols], vt_ref.at[g], blk)

        def prep(i, q_head=q_ref.at[:, cols]):
            q = q_head[i * blk:(i + 1) * blk, :].astype(F32)
            return jnp.concatenate([(q * (d ** -0.5 * LOG2E)).astype(BF16), ones], axis=1), None

        problems.append(_AttnProblem(ka_ref.at[g], vt_ref.at[g], s_ref.at[g],
                                     o_ref.at[:, cols], prep, bias))

    _block_attention(nb, blk, FOX_ISSUE, problems)


def _fox(qkv, f, batch, seq):
    t = qkv.shape[0]
    nh = FOX_HEADS
    blk = FOX_BLOCK
    width = FOX_GROUP * HEAD_DIM
    groups = nh // FOX_GROUP

    def head_spec(group):
        return pl.BlockSpec((seq, width), lambda b, h: (b, group * groups + h))

    return pl.pallas_call(
        _fox_kernel,
        grid=(batch, groups),
        in_specs=[head_spec(0), head_spec(1), head_spec(2),
                  pl.BlockSpec((nh, seq), lambda b, h: (0, b))],
        out_specs=pl.BlockSpec((seq, width), lambda b, h: (b, h)),
        out_shape=jax.ShapeDtypeStruct((t, nh * HEAD_DIM), BF16),
        scratch_shapes=[pltpu.VMEM((FOX_GROUP, seq, 2 * HEAD_DIM), BF16),
                        pltpu.VMEM((FOX_GROUP, HEAD_DIM + BF16_SUBLANES, seq), BF16),
                        pltpu.VMEM((FOX_GROUP, 2, seq, blk), F32)],
        compiler_params=_params("parallel", "parallel"),
        name="fox",
    )(qkv, qkv, qkv, f)


def _rotary_tables(seq):
    inv_freq = np.power(ROPE_BASE, -np.arange(0, HEAD_DIM, 2, dtype=np.float64) / HEAD_DIM)
    ang = np.arange(seq, dtype=np.float64)[:, None] * inv_freq[None, :]
    cos, sin = np.cos(ang), np.sin(ang)
    return (jnp.asarray(np.concatenate([cos, cos], axis=-1), F32),
            jnp.asarray(np.concatenate([-sin, sin], axis=-1), F32))


def _t5_bucket(rel):
    n = jnp.maximum(rel, 0)
    max_exact = N_REL_BUCKETS // 2
    nf = jnp.maximum(n, 1).astype(F32)
    large = max_exact + (jnp.log(nf / max_exact) / math.log(REL_MAX_DISTANCE / max_exact)
                         * (N_REL_BUCKETS - max_exact)).astype(jnp.int32)
    large = jnp.minimum(large, N_REL_BUCKETS - 1)
    return jnp.where(n < max_exact, n, large)


def _moba_bias_vectors(rel_bias):
    blk = MOBA_BLOCK
    bias_t = rel_bias.astype(F32).T
    hot = _t5_bucket(jnp.arange(2 * blk))[:, None] == jnp.arange(N_REL_BUCKETS)
    near = jnp.sum(jnp.where(hot[None], bias_t[:, None, :], 0.0), axis=-1)
    far = bias_t[:, _t5_bucket(jnp.array(2 * blk, jnp.int32))]
    return far, near[:, None, :]


def kernel(x, rel_bias, norm_mix_pre, norm_mix_post, norm_ffn_pre, norm_ffn_post,
           even_w_in, even_gn, even_w_out, odd_w_in, odd_b_forget, odd_w_out,
           ffn_w_in, ffn_w_out):
    batch, seq, d = x.shape
    depth = norm_mix_pre.shape[0]
    tokens = batch * seq
    assert tokens % ROW_TILE == 0 and tokens % MIX_ROWS == 0, (batch, seq)
    assert seq % MOBA_BLOCK == 0 and seq % FOX_BLOCK == 0 and seq % RET_CHUNK == 0, seq
    assert FOX_HEADS % FOX_GROUP == 0
    xt = x.reshape(tokens, d)

    cos2, sin2 = _rotary_tables(seq)
    log_g = jnp.log1p(-jnp.power(2.0, -5.0 - jnp.arange(RET_HEADS, dtype=F32)))
    b31, bvec = _moba_bias_vectors(rel_bias)

    def row(v):
        return v.reshape(1, -1).astype(F32)

    w_ffn_all = ffn_w_in.astype(BF16)
    w_o_all = ffn_w_out.astype(BF16)
    for layer in range(depth):
        i = layer // 2
        if layer % 2 == 0:
            proj = _norm_proj(xt, row(norm_mix_pre[layer]), even_w_in[i].astype(BF16))
            ret, mob = _even_mixer(proj, log_g, cos2, sin2, row(even_gn[i]), b31, bvec,
                                   batch, seq)
            acts, w_mix = [ret, mob], even_w_out[i]
        else:
            n_qkv = 3 * FOX_HEADS * HEAD_DIM
            w_gate = jnp.pad(odd_w_in[i][:, n_qkv:], ((0, 0), (0, HEAD_DIM - FOX_HEADS)))
            qkv, fl_t = _norm_proj_gate(xt, row(norm_mix_pre[layer]), odd_w_in[i].astype(BF16),
                                        w_gate.astype(BF16), n_qkv, FOX_HEADS)
            f = _fox_gate(fl_t, odd_b_forget[i].reshape(-1, 1).astype(F32), batch, seq)
            acts, w_mix = [_fox(qkv, f, batch, seq)], odd_w_out[i]
        xt = _mix_ffn(acts, w_mix.astype(BF16), row(norm_mix_post[layer]), xt,
                      row(norm_ffn_pre[layer]), w_ffn_all, w_o_all,
                      row(norm_ffn_post[layer]), layer)
    return xt.reshape(batch, seq, d)
```

```python
import functools
import math
from typing import Any, Callable, NamedTuple

import jax
import jax.numpy as jnp
import numpy as np
from jax import lax
from jax.experimental import pallas as pl
from jax.experimental.pallas import tpu as pltpu

F32 = jnp.float32
BF16 = jnp.bfloat16

HEAD_DIM = 128
RET_HEADS = 4
MOBA_HEADS = 4
FOX_HEADS = 8
RET_CHUNK = 128
MOBA_BLOCK = 256
MOBA_TOPK = 3
N_REL_BUCKETS = 32
REL_MAX_DISTANCE = 128
ROPE_BASE = 10000.0
RMS_EPS = 1e-6
GN_EPS = 1e-5
NEG = -1e30
LOG2E = math.log2(math.e)

V7X_VMEM_BYTES = 64 * 1024 * 1024
VMEM_LIMIT = V7X_VMEM_BYTES - 8 * 1024 * 1024

F32_SUBLANES = 8
BF16_SUBLANES = 16

ROW_TILE = 1024
MIX_ROWS = 1024
SUB_ROWS = 512
COL_CHUNK = 512
FFN_CHUNK = 256
FOX_BLOCK = 256
FOX_GROUP = 2
FOX_BIAS_TERMS = 3
MOBA_ISSUE = (2, 4)
FOX_ISSUE = (4, 4)
RET_FILL_EVERY = 1


def _params(*sem):
    return pltpu.CompilerParams(dimension_semantics=sem, vmem_limit_bytes=VMEM_LIMIT)


def _resident(shape):
    nd = len(shape)
    return pl.BlockSpec(shape, lambda *_: (0,) * nd, pipeline_mode=pl.Buffered(1))


def _rms(x, g):
    return x * lax.rsqrt(jnp.mean(x * x, axis=-1, keepdims=True) + RMS_EPS) * g


def _dot(a, b):
    return jnp.dot(a, b, preferred_element_type=F32)


def _dot_nt(a, b):
    return lax.dot_general(a, b, (((1,), (1,)), ((), ())), preferred_element_type=F32)


def _sub_tiles(rows, sub):
    return [slice(r, r + sub) for r in range(0, rows, sub)]


def _norm_proj_kernel(x_ref, g_ref, w_ref, o_ref):
    subs = _sub_tiles(x_ref.shape[0], SUB_ROWS)
    hns = [_rms(x_ref[rows, :], g_ref[...]).astype(BF16) for rows in subs]
    for rows, hn in zip(subs, hns):
        for c in range(o_ref.shape[1] // COL_CHUNK):
            sl = slice(c * COL_CHUNK, (c + 1) * COL_CHUNK)
            o_ref[rows, sl] = _dot(hn, w_ref[:, sl]).astype(o_ref.dtype)


def _norm_proj(x, g, w):
    t, d = x.shape
    n = w.shape[1]
    return pl.pallas_call(
        _norm_proj_kernel,
        grid=(t // ROW_TILE,),
        in_specs=[pl.BlockSpec((ROW_TILE, d), lambda i: (i, 0)),
                  _resident((1, d)), _resident((d, n))],
        out_specs=pl.BlockSpec((ROW_TILE, n), lambda i: (i, 0)),
        out_shape=jax.ShapeDtypeStruct((t, n), BF16),
        compiler_params=_params("parallel"),
        name="norm_proj",
    )(x, g, w)


def _norm_proj_gate_kernel(tiles_per_seq, x_ref, g_ref, w_ref, wf_ref, bf_ref, o_ref, f_ref,
                           carry_ref):
    n = o_ref.shape[1]
    nh = f_ref.shape[0]

    @pl.when(pl.program_id(0) % tiles_per_seq == 0)
    def _():
        carry_ref[...] = jnp.zeros(carry_ref.shape, F32)

    subs = _sub_tiles(x_ref.shape[0], SUB_ROWS)
    hns = [_rms(x_ref[rows, :], g_ref[...]).astype(BF16) for rows in subs]
    lane = lax.broadcasted_iota(jnp.int32, (nh, SUB_ROWS), 1)
    carry = carry_ref[:, 0:1]
    for rows, hn in zip(subs, hns):
        for c in range(n // COL_CHUNK):
            sl = slice(c * COL_CHUNK, (c + 1) * COL_CHUNK)
            o_ref[rows, sl] = _dot(hn, w_ref[:, sl]).astype(o_ref.dtype)
        z = jnp.transpose(_dot(hn, wf_ref[...]))[0:nh, :] + bf_ref[...]
        y = jnp.minimum(z, 0.0) - jnp.log1p(jnp.exp(-jnp.abs(z)))
        sh = 1
        while sh < SUB_ROWS:
            y = y + jnp.where(lane >= sh, pltpu.roll(y, sh, 1), 0.0)
            sh *= 2
        y = y + carry
        f_ref[:, rows] = y
        carry = y[:, SUB_ROWS - 1:SUB_ROWS]
    carry_ref[...] = jnp.broadcast_to(carry, carry_ref.shape)


def _norm_proj_gate(x, g, w, wf, b_forget, n, seq):
    t, d = x.shape
    nh = b_forget.shape[0]
    assert seq % ROW_TILE == 0
    return pl.pallas_call(
        functools.partial(_norm_proj_gate_kernel, seq // ROW_TILE),
        grid=(t // ROW_TILE,),
        in_specs=[pl.BlockSpec((ROW_TILE, d), lambda i: (i, 0)),
                  _resident((1, d)), _resident(w.shape), _resident(wf.shape),
                  _resident((nh, 1))],
        out_specs=[pl.BlockSpec((ROW_TILE, n), lambda i: (i, 0)),
                   pl.BlockSpec((nh, ROW_TILE), lambda i: (0, i))],
        out_shape=[jax.ShapeDtypeStruct((t, n), BF16),
                   jax.ShapeDtypeStruct((nh, t), F32)],
        scratch_shapes=[pltpu.VMEM((nh, HEAD_DIM), F32)],
        compiler_params=_params("arbitrary"),
        name="norm_proj_gate",
    )(x, g, w, wf, b_forget)


def _mix_ffn_kernel(n_in, *refs):
    a_refs = refs[:n_in]
    wmix_ref, gmix_ref, x_ref, gpre_ref, wffn_ref, wo_ref, gpost_ref, o_ref = refs[n_in:]
    hidden = wo_ref.shape[0]
    subs = _sub_tiles(x_ref.shape[0], SUB_ROWS)

    xs, hns = [], []
    for rows in subs:
        m = None
        off = 0
        for a_ref in a_refs:
            width = a_ref.shape[1]
            part = _dot(a_ref[rows, :], wmix_ref[off:off + width, :])
            m = part if m is None else m + part
            off += width
        x = x_ref[rows, :] + _rms(m, gmix_ref[...])
        xs.append(x)
        hns.append(_rms(x, gpre_ref[...]).astype(BF16))
    for rows, x, hn in zip(subs, xs, hns):
        y = jnp.zeros(x.shape, F32)
        for c in range(hidden // FFN_CHUNK):
            lo = c * FFN_CHUNK
            gate = _dot(hn, wffn_ref[:, lo:lo + FFN_CHUNK])
            up = _dot(hn, wffn_ref[:, hidden + lo:hidden + lo + FFN_CHUNK])
            act = (gate * jax.nn.sigmoid(gate) * up).astype(BF16)
            y = y + _dot(act, wo_ref[lo:lo + FFN_CHUNK, :])
        o_ref[rows, :] = x + _rms(y, gpost_ref[...])


def _mix_ffn(acts, w_mix, g_mix, x, g_pre, w_ffn, w_o, g_post, layer):
    t, d = x.shape

    def layer_slab(w):
        return pl.BlockSpec((None,) + w.shape[1:], lambda i: (layer, 0, 0),
                            pipeline_mode=pl.Buffered(1))

    in_specs = [pl.BlockSpec((MIX_ROWS, a.shape[1]), lambda i: (i, 0)) for a in acts]
    in_specs += [_resident(w_mix.shape), _resident((1, d)),
                 pl.BlockSpec((MIX_ROWS, d), lambda i: (i, 0)),
                 _resident((1, d)), layer_slab(w_ffn), layer_slab(w_o),
                 _resident((1, d))]
    return pl.pallas_call(
        functools.partial(_mix_ffn_kernel, len(acts)),
        grid=(t // MIX_ROWS,),
        in_specs=in_specs,
        out_specs=pl.BlockSpec((MIX_ROWS, d), lambda i: (i, 0)),
        out_shape=jax.ShapeDtypeStruct((t, d), F32),
        compiler_params=_params("parallel"),
        name="mix_ffn",
    )(*acts, w_mix, g_mix, x, g_pre, w_ffn, w_o, g_post)


def _retention_work(lg, q_ref, k_ref, v_ref, g_ref, cos_ref, sin_ref, gn_ref, o_ref):
    c = RET_CHUNK
    seq = q_ref.shape[0]
    row = lax.broadcasted_iota(jnp.int32, (c, c), 0).astype(F32)
    col = lax.broadcasted_iota(jnp.int32, (c, c), 1).astype(F32)
    diff = row - col
    inner = jnp.where(diff >= 0, jnp.exp(lg * jnp.maximum(diff, 0.0)), 0.0)
    idx = lax.broadcasted_iota(jnp.int32, (c, HEAD_DIM), 0).astype(F32)
    q_dec = jnp.exp(lg * (idx + 1.0))
    k_dec = jnp.exp(lg * (c - 1.0 - idx))
    chunk_dec = jnp.exp(lg * jnp.full((1, HEAD_DIM), float(c), F32))
    gn_w = gn_ref[...]
    scale = HEAD_DIM ** -0.5

    def rot(t, cos, sin):
        return t * cos + pltpu.roll(t, HEAD_DIM // 2, 1) * sin

    def head(n):
        sl = slice(n * c, (n + 1) * c)
        cos = cos_ref[sl, :]
        sin = sin_ref[sl, :]
        q = rot(q_ref[sl, :].astype(F32), cos, sin) * scale
        k = rot(k_ref[sl, :].astype(F32), cos, sin)
        qb = q.astype(BF16)
        return qb, k, _dot_nt(qb, k.astype(BF16))

    def tail(n, state, qb, k, sc):
        sl = slice(n * c, (n + 1) * c)
        v = v_ref[sl, :]
        kd_t = jnp.transpose(k * k_dec).astype(BF16)
        kv = _dot(kd_t, v)
        cross = _dot(qb, state.astype(BF16))
        o = _dot((sc * inner).astype(BF16), v) + cross * q_dec
        mu = jnp.mean(o, axis=-1, keepdims=True)
        var = jnp.mean(jnp.square(o - mu), axis=-1, keepdims=True)
        y = (o - mu) * lax.rsqrt(var + GN_EPS) * gn_w
        gate = g_ref[sl, :].astype(F32)
        o_ref[sl, :] = (y * (gate * jax.nn.sigmoid(gate))).astype(o_ref.dtype)
        return state * chunk_dec + kv

    n_chunks = seq // c
    work = {"n": 0, "state": jnp.zeros((HEAD_DIM, HEAD_DIM), F32), "ahead": head(0)}

    def emit():
        n = work["n"]
        if n >= n_chunks:
            return False
        cur = work["ahead"]
        if n + 1 < n_chunks:
            work["ahead"] = head(n + 1)
        work["state"] = tail(n, work["state"], *cur)
        work["n"] = n + 1
        return n + 1 < n_chunks

    return emit


def _group_reduce(x, op):
    r, c = x.shape
    s = F32_SUBLANES
    g = r // s
    if g % 4 == 0:
        return op(op(x.reshape(4, g // 4, s, c), axis=1), axis=0)
    return op(x.reshape(g, s, c), axis=0)


class _AttnProblem(NamedTuple):
    keys: Any
    vt: Any
    scores: Any
    out: Any
    prep: Callable
    bias: Callable


def _block_attention(nb, blk, issue, problems, filler=None, fill_every=1):
    d = HEAD_DIM
    chunk, lag = issue
    steps = 0

    def chunks(i):
        return [(lo, min(lo + chunk, i + 1)) for lo in range(0, i + 1, chunk)]

    def score_chunk(p, i, lo, hi, qa, aux, m8, snaps):
        s = _dot_nt(p.keys[lo * blk:hi * blk, :], qa)
        for j in range(lo, hi):
            t = s[(j - lo) * blk:(j - lo + 1) * blk, :]
            tile_bias, col_bias = p.bias(i, j, aux)
            if tile_bias is not None:
                t = t + tile_bias
            p.scores[i % 2, j * blk:(j + 1) * blk, :] = t
            t8 = _group_reduce(t, jnp.max)
            if col_bias is not None:
                t8 = t8 + col_bias
            m8 = t8 if m8 is None else jnp.maximum(m8, t8)
            snaps[j] = m8
        return m8

    def attend_chunk(p, i, lo, hi, m, aux, acc, snaps):
        ps = []
        for j in range(lo, hi):
            snap = snaps.get(j - lag)
            m_j = m if snap is None else jnp.minimum(m, jnp.maximum(m, snap[0:1, :]))
            col_bias = p.bias(i, j, aux)[1]
            if col_bias is not None:
                m_j = m_j - col_bias
            ps.append(jnp.exp2(p.scores[i % 2, j * blk:(j + 1) * blk, :] - m_j).astype(BF16))
        pm = ps[0] if len(ps) == 1 else jnp.concatenate(ps, axis=0)
        part = _dot(p.vt[:, lo * blk:hi * blk], pm)
        return part if acc is None else acc + part

    states = []
    for p in problems:
        qa, aux = p.prep(nb - 1)
        m8 = None
        for lo, hi in chunks(nb - 1):
            m8 = score_chunk(p, nb - 1, lo, hi, qa, aux, m8, {})
        states.append({"m8": m8, "aux_next": aux})
    for i in reversed(range(nb)):
        ahead = chunks(i - 1) if i > 0 else []
        here = chunks(i)
        for p, st in zip(problems, states):
            st["m"] = jnp.max(st["m8"], axis=0, keepdims=True)
            st["aux"] = st["aux_next"]
            st["ops"] = p.prep(i - 1) if i > 0 else None
            st["aux_next"] = st["ops"][1] if i > 0 else None
            st["m8"], st["acc"], st["snaps"] = None, None, {}
        for c in range(len(here)):
            for p, st in zip(problems, states):
                done = dict(st["snaps"])
                if c < len(ahead):
                    st["m8"] = score_chunk(p, i - 1, *ahead[c], *st["ops"], st["m8"],
                                           st["snaps"])
                st["acc"] = attend_chunk(p, i, *here[c], st["m"], st["aux"], st["acc"], done)
            steps += 1
            if filler is not None and steps % fill_every == 0:
                filler()
        for p, st in zip(problems, states):
            acc = st["acc"]
            out = acc[0:d, :] / acc[d:d + 1, :]
            p.out[i * blk:(i + 1) * blk, :] = jnp.transpose(out).astype(p.out.dtype)
    while filler is not None and filler():
        pass


def _fill_values(v_ref, vt_ref, blk):
    d = HEAD_DIM
    seq = v_ref.shape[0]
    for n in range(seq // blk):
        rows = slice(n * blk, (n + 1) * blk)
        vt_ref[0:d, rows] = jnp.transpose(v_ref[rows, :])
    first = lax.broadcasted_iota(jnp.int32, (vt_ref.shape[0] - d, seq), 0) == 0
    vt_ref[d:, :] = jnp.where(first, 1.0, 0.0).astype(BF16)


def _even_mixer_kernel(lg_ref, b31_ref, rq_ref, rk_ref, rv_ref, rg_ref, cos_ref, sin_ref,
                       gn_ref, q_ref, k_ref, v_ref, bvec_ref, ret_ref, o_ref, vt_ref, s_ref):
    blk = MOBA_BLOCK
    nb = k_ref.shape[0] // blk
    h = pl.program_id(1)
    b31 = b31_ref[h]
    retention_chunk = _retention_work(lg_ref[h], rq_ref, rk_ref, rv_ref, rg_ref,
                                      cos_ref, sin_ref, gn_ref, ret_ref)

    _fill_values(v_ref, vt_ref, blk)
    means = [jnp.mean(k_ref[n * blk:(n + 1) * blk, :].astype(F32), axis=0, keepdims=True)
             for n in range(nb)]
    km = jnp.concatenate(means, axis=0)
    km_hi = km.astype(BF16)
    km_lo = (km - km_hi.astype(F32)).astype(BF16)
    row = lax.broadcasted_iota(jnp.int32, (nb, blk), 0).astype(F32)
    bvec = jnp.broadcast_to(bvec_ref[...] * LOG2E, (blk, 2 * blk))
    toep = pltpu.roll(bvec, 0, 1, stride=1, stride_axis=0)
    krow = lax.broadcasted_iota(jnp.int32, (blk, blk), 0)
    qcol = lax.broadcasted_iota(jnp.int32, (blk, blk), 1)
    town = jnp.where(qcol >= krow, toep[:, 0:blk], NEG)
    tadj = toep[:, blk:2 * blk]

    def prep(i):
        q = q_ref[i * blk:(i + 1) * blk, :]
        gate = _dot_nt(km_hi, q) + _dot_nt(km_lo, q)
        valid = row < float(i)
        gm = jnp.where(valid, gate, -jnp.inf)
        sel = jnp.full(gate.shape, NEG, F32)
        for _ in range(min(MOBA_TOPK, nb)):
            mx = jnp.max(gm, axis=0, keepdims=True)
            first = jnp.min(jnp.where(gm == mx, row, float(nb)), axis=0, keepdims=True)
            pick = row == first
            sel = jnp.where(jnp.logical_and(pick, valid), 0.0, sel)
            gm = jnp.where(pick, -jnp.inf, gm)

        qs = (q.astype(F32) * (HEAD_DIM ** -0.5 * LOG2E)).astype(BF16)
        return qs, (sel, sel + b31 * LOG2E)

    def bias(i, j, aux):
        sel, far = aux
        if j == i:
            return town, None
        if j == i - 1:
            return tadj, sel[j:j + 1, :]
        return None, far[j:j + 1, :]

    _block_attention(nb, blk, MOBA_ISSUE,
                     [_AttnProblem(k_ref, vt_ref, s_ref, o_ref, prep, bias)],
                     filler=retention_chunk, fill_every=RET_FILL_EVERY)


def _even_mixer(proj, log_g, cos2, sin2, gn_w, b31, bvec, batch, seq):
    assert RET_HEADS == MOBA_HEADS
    t = proj.shape[0]
    nh = MOBA_HEADS
    blk = MOBA_BLOCK

    def head_spec(group):
        return pl.BlockSpec((seq, HEAD_DIM), lambda b, h: (b, group * nh + h))

    out_spec = pl.BlockSpec((seq, HEAD_DIM), lambda b, h: (b, h))
    out_shape = jax.ShapeDtypeStruct((t, nh * HEAD_DIM), BF16)
    return pl.pallas_call(
        _even_mixer_kernel,
        grid=(batch, nh),
        in_specs=[pl.BlockSpec(memory_space=pltpu.SMEM), pl.BlockSpec(memory_space=pltpu.SMEM),
                  head_spec(0), head_spec(1), head_spec(2), head_spec(3),
                  _resident((seq, HEAD_DIM)), _resident((seq, HEAD_DIM)),
                  pl.BlockSpec((1, HEAD_DIM), lambda b, h: (0, h)),
                  head_spec(4), head_spec(5), head_spec(6),
                  pl.BlockSpec((None, 1, 2 * blk), lambda b, h: (h, 0, 0))],
        out_specs=[out_spec, out_spec],
        out_shape=[out_shape, out_shape],
        scratch_shapes=[pltpu.VMEM((HEAD_DIM + BF16_SUBLANES, seq), BF16),
                        pltpu.VMEM((2, seq, blk), F32)],
        compiler_params=_params("parallel", "parallel"),
        name="even_mixer",
    )(log_g, b31, proj, proj, proj, proj, cos2, sin2, gn_w, proj, proj, proj, bvec)


def _fox_kernel(q_ref, k_ref, v_ref, f_ref, o_ref, ka_ref, vt_ref, s_ref):
    blk = FOX_BLOCK
    nb = k_ref.shape[0] // blk
    d = HEAD_DIM
    lane = lax.broadcasted_iota(jnp.int32, (blk, d), 1)
    ones = jnp.where(lane < FOX_BIAS_TERMS, 1.0, 0.0).astype(BF16)
    krow = lax.broadcasted_iota(jnp.int32, (blk, blk), 0)
    qcol = lax.broadcasted_iota(jnp.int32, (blk, blk), 1)
    causal = jnp.where(qcol >= krow, 0.0, NEG)
    pad = jnp.zeros((d - F32_SUBLANES, blk), F32)

    def bias(i, j, aux):
        return (causal if j == i else None), None

    problems = []
    for g in range(FOX_GROUP):
        cols = slice(g * d, (g + 1) * d)
        head = pl.program_id(1) * FOX_GROUP + g
        frow = f_ref[pl.ds(head, 1), :] * (-LOG2E)
        hi = frow.astype(BF16).astype(F32)
        mid = (frow - hi).astype(BF16).astype(F32)
        lo = frow - hi - mid
        sub = lax.broadcasted_iota(jnp.int32, (F32_SUBLANES, frow.shape[1]), 0)
        top = jnp.where(sub == 0, hi, jnp.where(sub == 1, mid, jnp.where(sub == 2, lo, 0.0)))
        for n in range(nb):
            rows = slice(n * blk, (n + 1) * blk)
            aug = jnp.transpose(jnp.concatenate([top[:, rows], pad], axis=0))
            ka_ref[g, rows, 0:d] = k_ref[rows, cols]
            ka_ref[g, rows, d:2 * d] = aug.astype(BF16)
        _fill_values(v_ref.at[:, cols], vt_ref.at[g], blk)

        def prep(i, q_head=q_ref.at[:, cols]):
            q = q_head[i * blk:(i + 1) * blk, :].astype(F32)
            return jnp.concatenate([(q * (d ** -0.5 * LOG2E)).astype(BF16), ones], axis=1), None

        problems.append(_AttnProblem(ka_ref.at[g], vt_ref.at[g], s_ref.at[g],
                                     o_ref.at[:, cols], prep, bias))

    _block_attention(nb, blk, FOX_ISSUE, problems)


def _fox(qkv, f, batch, seq):
    t = qkv.shape[0]
    nh = FOX_HEADS
    blk = FOX_BLOCK
    width = FOX_GROUP * HEAD_DIM
    groups = nh // FOX_GROUP

    def head_spec(group):
        return pl.BlockSpec((seq, width), lambda b, h: (b, group * groups + h))

    return pl.pallas_call(
        _fox_kernel,
        grid=(batch, groups),
        in_specs=[head_spec(0), head_spec(1), head_spec(2),
                  pl.BlockSpec((nh, seq), lambda b, h: (0, b))],
        out_specs=pl.BlockSpec((seq, width), lambda b, h: (b, h)),
        out_shape=jax.ShapeDtypeStruct((t, nh * HEAD_DIM), BF16),
        scratch_shapes=[pltpu.VMEM((FOX_GROUP, seq, 2 * HEAD_DIM), BF16),
                        pltpu.VMEM((FOX_GROUP, HEAD_DIM + BF16_SUBLANES, seq), BF16),
                        pltpu.VMEM((FOX_GROUP, 2, seq, blk), F32)],
        compiler_params=_params("parallel", "parallel"),
        name="fox",
    )(qkv, qkv, qkv, f)


def _rotary_tables(seq):
    inv_freq = np.power(ROPE_BASE, -np.arange(0, HEAD_DIM, 2, dtype=np.float64) / HEAD_DIM)
    ang = np.arange(seq, dtype=np.float64)[:, None] * inv_freq[None, :]
    cos, sin = np.cos(ang), np.sin(ang)
    return (jnp.asarray(np.concatenate([cos, cos], axis=-1), F32),
            jnp.asarray(np.concatenate([-sin, sin], axis=-1), F32))


def _t5_bucket(rel):
    n = jnp.maximum(rel, 0)
    max_exact = N_REL_BUCKETS // 2
    nf = jnp.maximum(n, 1).astype(F32)
    large = max_exact + (jnp.log(nf / max_exact) / math.log(REL_MAX_DISTANCE / max_exact)
                         * (N_REL_BUCKETS - max_exact)).astype(jnp.int32)
    large = jnp.minimum(large, N_REL_BUCKETS - 1)
    return jnp.where(n < max_exact, n, large)


def _moba_bias_vectors(rel_bias):
    blk = MOBA_BLOCK
    bias_t = rel_bias.astype(F32).T
    hot = _t5_bucket(jnp.arange(2 * blk))[:, None] == jnp.arange(N_REL_BUCKETS)
    near = jnp.sum(jnp.where(hot[None], bias_t[:, None, :], 0.0), axis=-1)
    far = bias_t[:, _t5_bucket(jnp.array(2 * blk, jnp.int32))]
    return far, near[:, None, :]


def kernel(x, rel_bias, norm_mix_pre, norm_mix_post, norm_ffn_pre, norm_ffn_post,
           even_w_in, even_gn, even_w_out, odd_w_in, odd_b_forget, odd_w_out,
           ffn_w_in, ffn_w_out):
    batch, seq, d = x.shape
    depth = norm_mix_pre.shape[0]
    tokens = batch * seq
    assert tokens % ROW_TILE == 0 and tokens % MIX_ROWS == 0, (batch, seq)
    assert seq % MOBA_BLOCK == 0 and seq % FOX_BLOCK == 0 and seq % RET_CHUNK == 0, seq
    assert FOX_HEADS % FOX_GROUP == 0
    xt = x.reshape(tokens, d)

    cos2, sin2 = _rotary_tables(seq)
    log_g = jnp.log1p(-jnp.power(2.0, -5.0 - jnp.arange(RET_HEADS, dtype=F32)))
    b31, bvec = _moba_bias_vectors(rel_bias)

    def row(v):
        return v.reshape(1, -1).astype(F32)

    w_ffn_all = ffn_w_in.astype(BF16)
    w_o_all = ffn_w_out.astype(BF16)
    for layer in range(depth):
        i = layer // 2
        if layer % 2 == 0:
            proj = _norm_proj(xt, row(norm_mix_pre[layer]), even_w_in[i].astype(BF16))
            ret, mob = _even_mixer(proj, log_g, cos2, sin2, row(even_gn[i]), b31, bvec,
                                   batch, seq)
            acts, w_mix = [ret, mob], even_w_out[i]
        else:
            n_qkv = 3 * FOX_HEADS * HEAD_DIM
            w_gate = jnp.pad(odd_w_in[i][:, n_qkv:], ((0, 0), (0, HEAD_DIM - FOX_HEADS)))
            qkv, f = _norm_proj_gate(xt, row(norm_mix_pre[layer]), odd_w_in[i].astype(BF16),
                                     w_gate.astype(BF16),
                                     odd_b_forget[i].reshape(-1, 1).astype(F32), n_qkv, seq)
            acts, w_mix = [_fox(qkv, f, batch, seq)], odd_w_out[i]
        xt = _mix_ffn(acts, w_mix.astype(BF16), row(norm_mix_post[layer]), xt,
                      row(norm_ffn_pre[layer]), w_ffn_all, w_o_all,
                      row(norm_ffn_post[layer]), layer)
    return xt.reshape(batch, seq, d)
```
